```python
import jax, jax.numpy as jnp
from jax import lax
import numpy as np

D_MODEL = 1024
BATCH = 2
SEQ = 8192
DEPTH = 2

CONV_WIDTH = D_MODEL
CONV_K = 3
ATTN_GROUPS = ((128, 1), (512, 4), (2048, 16))
HEADS_PER_GROUP = 8
HEAD_DIM = 64
ATTN_BLK = 128
ATTN_OUT = HEADS_PER_GROUP * HEAD_DIM
ROPE_THETA = 10000.0
GMLP_WIDTH = D_MODEL
GMLP_GROUPS = 8
GMLP_GROUP_DIM = GMLP_WIDTH // GMLP_GROUPS
GMLP_CHUNK = 128
D_FF = ((-(-8 * D_MODEL // 3)) + 255) // 256 * 256
ALPHA = (2 * DEPTH) ** 0.25
BETA = (8 * DEPTH) ** -0.25
LN_EPS = 1e-5

GROUP_COL = HEADS_PER_GROUP * HEAD_DIM
SPLIT_SIZES = ([D_MODEL] * 3 + [CONV_WIDTH] * 3 + [GROUP_COL] * (3 * len(ATTN_GROUPS)) + [GMLP_WIDTH] * 2)
N_IN = sum(SPLIT_SIZES)

kernel_name = 'hybrid_conv_dilattn_gmlp_deepnorm'


def layer_norm(x, g, b):
    x32 = x.astype(jnp.float32)
    mu = jnp.mean(x32, axis=-1, keepdims=True)
    var = jnp.mean(jnp.square(x32 - mu), axis=-1, keepdims=True)
    y = (x32 - mu) * lax.rsqrt(var + LN_EPS) * g.astype(jnp.float32) + b.astype(jnp.float32)
    return y.astype(x.dtype)


def rope(t, positions):
    half = HEAD_DIM // 2
    inv_freq = ROPE_THETA ** (-jnp.arange(half, dtype=jnp.float32) / half)
    ang = positions.astype(jnp.float32)[..., None] * inv_freq
    cos = jnp.cos(ang)[:, :, None, :]
    sin = jnp.sin(ang)[:, :, None, :]
    t32 = t.astype(jnp.float32)
    t1, t2 = t32[..., :half], t32[..., half:]
    return jnp.concatenate([t1 * cos - t2 * sin, t2 * cos + t1 * sin], axis=-1).astype(t.dtype)


def short_conv_mixer(b_gate, c_gate, h, conv_w):
    z = c_gate * h
    conv = lax.conv_general_dilated(
        z, conv_w[:, None, :].astype(z.dtype), window_strides=(1,), padding=[(CONV_K - 1, 0)],
        dimension_numbers=('NWC', 'WIO', 'NWC'), feature_group_count=CONV_WIDTH)
    return b_gate * conv


def dilated_window_attention(q, k, v, window, dil):
    B, S, H, Dh = q.shape
    steps = window // dil
    span = dil * ATTN_BLK
    s_pad = -(-S // span) * span
    L = s_pad // dil
    nb = L // ATTN_BLK
    pad = ((0, 0), (0, s_pad - S), (0, 0), (0, 0))

    def fold(a):
        a = jnp.pad(a, pad).reshape(B, L, dil, H, Dh).transpose(0, 2, 1, 3, 4)
        return a.reshape(B, dil, nb, ATTN_BLK, H, Dh)

    qb, kb, vb = fold(q), fold(k), fold(v)
    blk_pad = ((0, 0), (0, 0), (1, 0), (0, 0), (0, 0), (0, 0))
    kw = jnp.concatenate([jnp.pad(kb, blk_pad)[:, :, :-1], kb], axis=3)
    vw = jnp.concatenate([jnp.pad(vb, blk_pad)[:, :, :-1], vb], axis=3)
    s = jnp.einsum('brnqhd,brnkhd->brnhqk', qb, kw).astype(jnp.float32) * (HEAD_DIM ** -0.5)
    qi = jnp.arange(ATTN_BLK)[:, None] + ATTN_BLK
    ki = jnp.arange(2 * ATTN_BLK)[None, :]
    dist = qi - ki
    band = (dist >= 0) & (dist <= steps)
    first = (jnp.arange(nb)[:, None, None] == 0) & (ki[None] < ATTN_BLK)
    valid = band[None] & jnp.logical_not(first)
    s = jnp.where(valid[None, None, :, None], s, -jnp.inf)
    lse = jax.nn.logsumexp(s, axis=-1)
    p = jnp.exp(s - lse[..., None]).astype(v.dtype)
    o = jnp.einsum('brnhqk,brnkhd->brnqhd', p, vw)
    o = o.reshape(B, dil, L, H, Dh).transpose(0, 2, 1, 3, 4).reshape(B, s_pad, H, Dh)[:, :S]
    lse = lse.transpose(0, 1, 2, 4, 3).reshape(B, dil, L, H).transpose(0, 2, 1, 3).reshape(B, s_pad, H)[:, :S]
    return o, lse


def dilated_attention_mixer(qkv_parts, positions):
    B, S, _ = qkv_parts[0].shape
    outs, lses = [], []
    for g, (window, dil) in enumerate(ATTN_GROUPS):
        q = rope(qkv_parts[3 * g].reshape(B, S, HEADS_PER_GROUP, HEAD_DIM), positions)
        k = rope(qkv_parts[3 * g + 1].reshape(B, S, HEADS_PER_GROUP, HEAD_DIM), positions)
        v = qkv_parts[3 * g + 2].reshape(B, S, HEADS_PER_GROUP, HEAD_DIM)
        o, lse = dilated_window_attention(q, k, v, window, dil)
        outs.append(o)
        lses.append(lse)
    wts = jax.nn.softmax(jnp.stack(lses, axis=0), axis=0)
    o = sum(wts[g][..., None].astype(outs[g].dtype) * outs[g] for g in range(len(ATTN_GROUPS)))
    return o.reshape(B, S, ATTN_OUT)


def chunked_spatial_gating(u_pre, v_pre, ln_g, ln_b, w_s, b_s):
    B, S, _ = u_pre.shape
    u = jax.nn.gelu(u_pre, approximate=False)
    v = layer_norm(jax.nn.gelu(v_pre, approximate=False), ln_g, ln_b)
    n = S // GMLP_CHUNK
    vc = v.reshape(B, n, GMLP_CHUNK, GMLP_GROUPS, GMLP_GROUP_DIM)
    tril = jnp.tril(jnp.ones((GMLP_CHUNK, GMLP_CHUNK), dtype=w_s.dtype))
    w_causal = w_s * tril[None]
    sp = jnp.einsum('gij,bnjgc->bnigc', w_causal, vc) + b_s.T[None, None, :, :, None]
    return u * sp.reshape(B, S, GMLP_WIDTH)


def setup_inputs(seed: int = 0) -> dict:
    key = jax.random.key(seed)
    ks = jax.random.split(key, 24)
    f32 = jnp.float32

    def nrm(k, shape, fan_in, scale=1.0):
        return jax.random.normal(k, shape, f32) * (scale * fan_in ** -0.5)

    col_scale = np.ones((N_IN,), np.float32)
    off = 3 * D_MODEL + 3 * CONV_WIDTH
    for g in range(len(ATTN_GROUPS)):
        vs = off + (3 * g + 2) * GROUP_COL
        col_scale[vs:vs + GROUP_COL] = BETA
    x = jax.random.normal(ks[0], (BATCH, SEQ, D_MODEL), f32)
    offset = jax.random.randint(ks[1], (BATCH, 1), 0, 4096, dtype=jnp.int32)
    positions = (offset + jnp.arange(SEQ, dtype=jnp.int32)[None, :]).astype(jnp.int32)
    return {
        'x': x,
        'positions': positions,
        'w_in': nrm(ks[2], (DEPTH, D_MODEL, N_IN), D_MODEL) * jnp.asarray(col_scale),
        'conv_w': nrm(ks[3], (DEPTH, CONV_K, CONV_WIDTH), CONV_K),
        'gmlp_ln_g': 1.0 + 0.02 * jax.random.normal(ks[4], (DEPTH, GMLP_WIDTH), f32),
        'gmlp_ln_b': 0.02 * jax.random.normal(ks[5], (DEPTH, GMLP_WIDTH), f32),
        'w_s': nrm(ks[6], (DEPTH, GMLP_GROUPS, GMLP_CHUNK, GMLP_CHUNK), GMLP_CHUNK),
        'b_s': 1.0 + 0.1 * jax.random.normal(ks[7], (DEPTH, GMLP_GROUPS, GMLP_CHUNK), f32),
        'p_a': nrm(ks[8], (DEPTH, CONV_WIDTH, D_MODEL), CONV_WIDTH, BETA),
        'p_b': nrm(ks[9], (DEPTH, ATTN_OUT, D_MODEL), ATTN_OUT, BETA),
        'p_c': nrm(ks[10], (DEPTH, GMLP_WIDTH, D_MODEL), GMLP_WIDTH, BETA),
        'w_o': nrm(ks[11], (DEPTH, D_MODEL, D_MODEL), D_MODEL, BETA),
        'ln1_g': 1.0 + 0.02 * jax.random.normal(ks[12], (DEPTH, D_MODEL), f32),
        'ln1_b': 0.02 * jax.random.normal(ks[13], (DEPTH, D_MODEL), f32),
        'w_gate': nrm(ks[14], (DEPTH, D_MODEL, D_FF), D_MODEL),
        'w_up': nrm(ks[15], (DEPTH, D_MODEL, D_FF), D_MODEL, BETA),
        'w_down': nrm(ks[16], (DEPTH, D_FF, D_MODEL), D_FF, BETA),
        'ln2_g': 1.0 + 0.02 * jax.random.normal(ks[17], (DEPTH, D_MODEL), f32),
        'ln2_b': 0.02 * jax.random.normal(ks[18], (DEPTH, D_MODEL), f32),
    }


def reference(x, positions, w_in, conv_w, gmlp_ln_g, gmlp_ln_b, w_s, b_s, p_a, p_b, p_c, w_o,
              ln1_g, ln1_b, w_gate, w_up, w_down, ln2_g, ln2_b):
    split_points = np.cumsum(np.array(SPLIT_SIZES))[:-1].tolist()
    n_attn = 3 * len(ATTN_GROUPS)
    for l in range(DEPTH):
        proj = x @ w_in[l].astype(x.dtype)
        parts = jnp.split(proj, split_points, axis=-1)
        g_a, g_b, g_c = (jax.nn.sigmoid(p) for p in parts[0:3])
        y_a = short_conv_mixer(parts[3], parts[4], parts[5], conv_w[l])
        y_b = dilated_attention_mixer(parts[6:6 + n_attn], positions)
        y_c = chunked_spatial_gating(parts[6 + n_attn], parts[7 + n_attn], gmlp_ln_g[l], gmlp_ln_b[l],
                                     w_s[l], b_s[l])
        m = g_a * (y_a @ p_a[l]) + g_b * (y_b @ p_b[l]) + g_c * (y_c @ p_c[l])
        x = layer_norm(ALPHA * x + m @ w_o[l], ln1_g[l], ln1_b[l])
        h = jax.nn.silu(x @ w_gate[l]) * (x @ w_up[l])
        x = layer_norm(ALPHA * x + h @ w_down[l], ln2_g[l], ln2_b[l])
    return x
```

```python
import functools

import numpy as np
import jax
import jax.numpy as jnp
from jax import lax
from jax.experimental import pallas as pl
from jax.experimental.pallas import tpu as pltpu

F32 = jnp.float32
BF16 = jnp.bfloat16

D_MODEL = 1024
HEADS = 8
HEAD_DIM = 64
HALF = HEAD_DIM // 2
GROUP_COL = HEADS * HEAD_DIM
DILATIONS = (1, 4, 16)
N_GROUPS = len(DILATIONS)
ATTN_BLK = 128
N_RES = 16
CHUNK = N_RES * ATTN_BLK
GMLP_GROUPS = 8
GMLP_CHUNK = 128
D_FF = 2816
DEPTH = 2
ALPHA = (2 * DEPTH) ** 0.25
LN_EPS = 1e-5
ROPE_THETA = 10000.0

TILE = 512
ROWS_PER_RES = TILE // N_RES
LANES = 128
SUBLANES = 8
NEG = -1e30
VMEM_LIMIT = 56 * 1024 * 1024

_OFF_GATES = 0
_OFF_CONV = 3 * D_MODEL
_OFF_ATTN = 6 * D_MODEL
_OFF_GMLP = _OFF_ATTN + 3 * N_GROUPS * GROUP_COL


def _resident(shape):
    nd = len(shape)
    return pl.BlockSpec(shape, lambda *_: (0,) * nd, pipeline_mode=pl.Buffered(1))


def _layer_norm(v, g, b):
    mu = jnp.mean(v, axis=-1, keepdims=True)
    c = v - mu
    var = jnp.mean(c * c, axis=-1, keepdims=True)
    return c * lax.rsqrt(var + LN_EPS) * g + b


def _gelu(v):
    return 0.5 * v * (1.0 + lax.erf(v * 0.7071067811865476))


def _rope_table_kernel(pos_ref, inv_ref, cos_ref, sin_ref):
    ang = pos_ref[...] * inv_ref[...]
    cos_ref[...] = jnp.cos(ang)
    sin_ref[...] = jnp.sin(ang)


def _rope_tables(pos_col, inv_row):
    t = pos_col.shape[0]
    rows = 2048
    return pl.pallas_call(
        _rope_table_kernel,
        out_shape=(jax.ShapeDtypeStruct((t, LANES), F32),) * 2,
        grid=(t // rows,),
        in_specs=[pl.BlockSpec((rows, 1), lambda i: (i, 0)), _resident((1, LANES))],
        out_specs=(pl.BlockSpec((rows, LANES), lambda i: (i, 0)),) * 2,
        compiler_params=pltpu.CompilerParams(dimension_semantics=("arbitrary",)),
        name="rope_tables",
    )(pos_col, inv_row)


def _qkv_kernel(*refs):
    x_refs = refs[:8]
    w_ref, cos_ref, sin_ref = refs[8:11]
    out_refs = refs[11:]
    rows = [jnp.concatenate([xr[pl.ds(rho, ROWS_PER_RES, stride=N_RES), :] for xr in x_refs], axis=1)
            for rho in range(N_RES)]
    xp = jnp.concatenate(rows, axis=0).astype(BF16)
    cos = cos_ref[...]
    sin = sin_ref[...]
    scale = HEAD_DIM ** -0.5
    tables = ((cos * scale, sin * scale), (cos, sin))

    for g in range(N_GROUPS):
        for j in range(3):
            col = (3 * g + j) * GROUP_COL
            y = jnp.dot(xp, w_ref[:, col:col + GROUP_COL], preferred_element_type=F32)
            if j < 2:
                c, s = tables[j]
                parts = []
                for half in range(2):
                    a = y[:, 256 * half:256 * half + LANES]
                    b = y[:, 256 * half + LANES:256 * half + 2 * LANES]
                    parts += [a * c - b * s, b * c + a * s]
                y = jnp.concatenate(parts, axis=1)
            o_ref = out_refs[3 * g + j]
            if g == 0:
                pieces = [y[32 * rho + 8 * c:32 * rho + 8 * c + 8, :] for c in range(4) for rho in range(N_RES)]
                o_ref[...] = jnp.concatenate(pieces, axis=0).astype(BF16)
            elif g == 1:
                for a in range(4):
                    slab = jnp.concatenate([y[32 * (4 * b + a):32 * (4 * b + a) + 32, :] for b in range(4)], axis=0)
                    o_ref[:, GROUP_COL * a:GROUP_COL * (a + 1)] = slab.astype(BF16)
            else:
                for rho in range(N_RES):
                    o_ref[:, GROUP_COL * rho:GROUP_COL * (rho + 1)] = y[32 * rho:32 * rho + 32, :].astype(BF16)


def _qkv_call(x2d, w_qkv, cos_t, sin_t):
    t = x2d.shape[0]
    nt = t // TILE
    x_specs = [pl.BlockSpec((TILE, LANES), functools.partial(lambda i, c: (i, c), c=c)) for c in range(8)]
    out_shapes, out_specs = [], []
    for d in DILATIONS:
        for _ in range(3):
            out_shapes.append(jax.ShapeDtypeStruct((t // d, d * GROUP_COL), BF16))
            out_specs.append(pl.BlockSpec((TILE // d, d * GROUP_COL), lambda i: (i, 0)))
    return pl.pallas_call(
        _qkv_kernel,
        out_shape=tuple(out_shapes),
        grid=(nt,),
        in_specs=x_specs + [_resident(w_qkv.shape),
                            pl.BlockSpec((TILE, LANES), lambda i: (i, 0)),
                            pl.BlockSpec((TILE, LANES), lambda i: (i, 0))],
        out_specs=tuple(out_specs),
        compiler_params=pltpu.CompilerParams(dimension_semantics=("arbitrary",), vmem_limit_bytes=VMEM_LIMIT),
        name="qkv_rope",
    )(*([x2d] * 8), w_qkv, cos_t, sin_t)


def _mask16(cond):
    return jnp.where(cond, 1.0, 0.0).astype(BF16) > 0


def _value_rhs(v):
    lo = _mask16(lax.broadcasted_iota(jnp.int32, v.shape, 1) < HEAD_DIM)
    zero = jnp.zeros_like(v)
    one = jnp.ones_like(v)
    top = jnp.concatenate([jnp.where(lo, v, zero), jnp.where(lo, one, zero)], axis=1)
    bot = jnp.concatenate([jnp.where(lo, zero, v), jnp.where(lo, zero, one)], axis=1)
    return jnp.concatenate([top, bot], axis=0)


def _attn_block(q, kp, kc, vp, vc, bias_p, bias_c, state_get, state_put, emit):
    lane256 = lax.broadcasted_iota(jnp.int32, (ATTN_BLK, 2 * LANES), 1)
    head_of_lane = (lane256 % LANES) // HALF
    head_masks = [_mask16(head_of_lane == hh) for hh in range(4)]
    lo = lax.broadcasted_iota(jnp.int32, (1, LANES), 1) < HEAD_DIM
    for half in range(2):
        qh = q[:, 256 * half:256 * half + 256]
        kh = jnp.concatenate([kp[:, 256 * half:256 * half + 256], kc[:, 256 * half:256 * half + 256]], axis=0)
        lhs = jnp.concatenate([jnp.where(head_masks[hh], qh, jnp.zeros_like(qh)) for hh in range(4)], axis=0)
        scores = lax.dot_general(lhs, kh, (((1,), (1,)), ((), ())), preferred_element_type=F32)
        for pp in range(2):
            j = 2 * half + pp
            probs, maxes = [], []
            for hl in range(2):
                hh = 2 * pp + hl
                sp = scores[ATTN_BLK * hh:ATTN_BLK * (hh + 1), :ATTN_BLK] + bias_p
                sc = scores[ATTN_BLK * hh:ATTN_BLK * (hh + 1), ATTN_BLK:] + bias_c
                mb = jnp.max(jnp.maximum(sp, sc), axis=1, keepdims=True)
                probs.append((jnp.exp(sp - mb).astype(BF16), jnp.exp(sc - mb).astype(BF16)))
                maxes.append(mb)
            lhs_pv = jnp.concatenate([probs[0][0], probs[1][0], probs[0][1], probs[1][1]], axis=1)
            rhs = jnp.concatenate([_value_rhs(vp[:, LANES * j:LANES * (j + 1)]),
                                   _value_rhs(vc[:, LANES * j:LANES * (j + 1)])], axis=0)
            ol = jnp.dot(lhs_pv, rhs, preferred_element_type=F32)
            o_new, l_new = ol[:, :LANES], ol[:, LANES:]
            m_new = jnp.where(lo, maxes[0], maxes[1])
            old = state_get(j)
            if old is not None:
                acc0, l0, m0 = old
                m = jnp.maximum(m0, m_new)
                a0 = jnp.exp(m0 - m)
                a1 = jnp.exp(m_new - m)
                o_new = acc0 * a0 + o_new * a1
                l_new = l0 * a0 + l_new * a1
                m_new = m
            if emit is not None:
                emit(j, o_new / l_new)
            else:
                state_put(j, o_new, l_new, m_new)


def _attn_kernel(q1, k1, v1, kp1, vp1, q2, k2, v2, kp2, vp2, q3, k3, v3, kp3, vp3, mask_ref, out_ref,
                 acc_s, l_s, m_s):
    n = pl.program_id(1)
    s = pl.program_id(2)
    state = (acc_s, l_s, m_s)

    def lanes(j):
        return slice(LANES * j, LANES * (j + 1))

    @pl.when(s < 4)
    def _group1():
        for c in range(4):
            k_blk = pl.multiple_of((4 * s + c) * SUBLANES, SUBLANES)

            def put(j, *vals, k_blk=k_blk):
                for ref, val in zip(state, vals):
                    for rho in range(N_RES):
                        ref[rho, pl.ds(k_blk, SUBLANES), lanes(j)] = val[SUBLANES * rho:SUBLANES * (rho + 1), :]

            rows = slice(ATTN_BLK * c, ATTN_BLK * (c + 1))
            prev = slice(ATTN_BLK * (c - 1), ATTN_BLK * c)
            bias_p = mask_ref[0, :, :ATTN_BLK]
            if c == 0:
                bias_p = jnp.where(jnp.logical_and(n == 0, s == 0), NEG, bias_p)
            _attn_block(q1[rows, :], kp1[...] if c == 0 else k1[prev, :], k1[rows, :],
                        vp1[...] if c == 0 else v1[prev, :], v1[rows, :],
                        bias_p, mask_ref[0, :, ATTN_BLK:], lambda j: None, put, None)

    @pl.when(jnp.logical_and(s >= 4, s < 8))
    def _group2():
        t = s - 4
        base = pl.multiple_of(t * 32, 32)
        first = jnp.logical_and(n == 0, t == 0)
        for a in range(4):
            def get(j, a=a):
                return tuple(jnp.concatenate([ref[4 * b + a, pl.ds(base, 32), lanes(j)] for b in range(4)], axis=0)
                             for ref in state)

            def put(j, *vals, a=a):
                for ref, val in zip(state, vals):
                    for b in range(4):
                        ref[4 * b + a, pl.ds(base, 32), lanes(j)] = val[32 * b:32 * (b + 1), :]

            cols = slice(GROUP_COL * a, GROUP_COL * (a + 1))
            bias_p = jnp.where(first, NEG, mask_ref[1, :, :ATTN_BLK])
            _attn_block(q2[:, cols], kp2[:, cols], k2[:, cols], vp2[:, cols], v2[:, cols],
                        bias_p, mask_ref[1, :, ATTN_BLK:], get, put, None)

    @pl.when(s >= 8)
    def _group3():
        t = s - 8
        for c in range(4):
            rho = 4 * t + c

            def get(j, rho=rho):
                return tuple(ref[rho, :, lanes(j)] for ref in state)

            def emit(j, y, c=c):
                out_ref[0, c, :, lanes(j)] = y.astype(BF16)

            cols = slice(GROUP_COL * c, GROUP_COL * (c + 1))
            bias_p = jnp.where(n == 0, NEG, mask_ref[2, :, :ATTN_BLK])
            _attn_block(q3[:, cols], kp3[:, cols], k3[:, cols], vp3[:, cols], v3[:, cols],
                        bias_p, mask_ref[2, :, ATTN_BLK:], get, None, emit)


def _attn_call(qkv, masks, batch, seq):
    nchunk = seq // CHUNK
    tiles = seq // TILE

    def step(s, g):
        return jnp.clip(s - 4 * g, 0, 3)

    def cur1(b, n, s):
        return (b * tiles + 4 * n + step(s, 0), 0)

    def prev1(b, n, s):
        return (jnp.maximum(4 * (b * tiles + 4 * n + step(s, 0)) - 1, 0), 0)

    def cur2(b, n, s):
        return (b * tiles + 4 * n + step(s, 1), 0)

    def prev2(b, n, s):
        return (jnp.maximum(b * tiles + 4 * n + step(s, 1) - 1, 0), 0)

    def cur3(b, n, s):
        return (b * nchunk + n, step(s, 2))

    def prev3(b, n, s):
        return (jnp.maximum(b * nchunk + n - 1, 0), step(s, 2))

    big = (ATTN_BLK, 4 * GROUP_COL)
    in_specs = [
        pl.BlockSpec((TILE, GROUP_COL), cur1), pl.BlockSpec((TILE, GROUP_COL), cur1), pl.BlockSpec((TILE, GROUP_COL), cur1),
        pl.BlockSpec((ATTN_BLK, GROUP_COL), prev1), pl.BlockSpec((ATTN_BLK, GROUP_COL), prev1),
        pl.BlockSpec(big, cur2), pl.BlockSpec(big, cur2), pl.BlockSpec(big, cur2),
        pl.BlockSpec(big, prev2), pl.BlockSpec(big, prev2),
        pl.BlockSpec(big, cur3), pl.BlockSpec(big, cur3), pl.BlockSpec(big, cur3),
        pl.BlockSpec(big, prev3), pl.BlockSpec(big, prev3),
        _resident(masks.shape),
    ]
    q1, k1, v1, q2, k2, v2, q3, k3, v3 = qkv
    return pl.pallas_call(
        _attn_kernel,
        out_shape=jax.ShapeDtypeStruct((batch, N_RES, seq // N_RES, GROUP_COL), BF16),
        grid=(batch, nchunk, 12),
        in_specs=in_specs,
        out_specs=pl.BlockSpec((1, 4, ATTN_BLK, GROUP_COL), lambda b, n, s: (b, step(s, 2), n, 0)),
        scratch_shapes=[pltpu.VMEM((N_RES, ATTN_BLK, GROUP_COL), F32)] * 3,
        compiler_params=pltpu.CompilerParams(dimension_semantics=("arbitrary",) * 3, vmem_limit_bytes=VMEM_LIMIT),
        name="dilated_attention",
    )(q1, k1, v1, k1, v1, q2, k2, v2, k2, v2, q3, k3, v3, k3, v3, masks)


def _mix_kernel(x_ref, y_ref, w_ref, convw_ref, lng_ref, lnb_ref, ws_ref, bsb_ref, pa_ref, pb_ref, pc_ref, wo_ref,
                ln1g_ref, ln1b_ref, o_ref, zc_ref, ys_ref, *, tiles_per_batch):
    i = pl.program_id(0)
    x = x_ref[...]
    xb = x.astype(BF16)

    def proj(k):
        return jnp.dot(xb, w_ref[:, D_MODEL * k:D_MODEL * (k + 1)], preferred_element_type=F32)

    @pl.when(i % tiles_per_batch == 0)
    def _():
        zc_ref[...] = jnp.zeros_like(zc_ref)

    z = proj(4) * proj(5)
    head = jnp.concatenate([zc_ref[...], z[:SUBLANES, :]], axis=0)
    zc_ref[...] = z[TILE - SUBLANES:, :]
    conv = convw_ref[2:3, :] * z
    for shift in (1, 2):
        zs = jnp.concatenate([pltpu.roll(head, shift, 0)[SUBLANES:, :], pltpu.roll(z, shift, 0)[SUBLANES:, :]], axis=0)
        conv = conv + convw_ref[2 - shift:3 - shift, :] * zs
    ya = (proj(3) * conv).astype(BF16)
    m = jax.nn.sigmoid(proj(0)) * jnp.dot(ya, pa_ref[...], preferred_element_type=F32)

    u = _gelu(proj(6))
    vb = _layer_norm(_gelu(proj(7)), lng_ref[...], lnb_ref[...]).astype(BF16)
    n_chunks = TILE // GMLP_CHUNK
    tril = _mask16(lax.broadcasted_iota(jnp.int32, (GMLP_CHUNK, GMLP_CHUNK), 0)
                   >= lax.broadcasted_iota(jnp.int32, (GMLP_CHUNK, GMLP_CHUNK), 1))
    cols = []
    for g in range(GMLP_GROUPS):
        gl = slice(LANES * g, LANES * (g + 1))
        wg = jnp.where(tril, ws_ref[g], jnp.zeros((GMLP_CHUNK, GMLP_CHUNK), BF16))
        rhs = jnp.concatenate([vb[GMLP_CHUNK * c:GMLP_CHUNK * (c + 1), gl] for c in range(n_chunks)], axis=1)
        sp = jnp.dot(wg, rhs, preferred_element_type=F32)
        bias = bsb_ref[:, gl]
        cols.append(jnp.concatenate([sp[:, LANES * c:LANES * (c + 1)] + bias for c in range(n_chunks)], axis=0))
    yc = (u * jnp.concatenate(cols, axis=1)).astype(BF16)
    m = m + jax.nn.sigmoid(proj(2)) * jnp.dot(yc, pc_ref[...], preferred_element_type=F32)

    for rho in range(N_RES):
        piece = y_ref[0, rho].astype(F32)
        for sl in range(GROUP_COL // LANES):
            ys_ref[sl, pl.ds(rho, ROWS_PER_RES, stride=N_RES), :] = piece[:, LANES * sl:LANES * (sl + 1)]
    yb = jnp.concatenate([ys_ref[sl] for sl in range(GROUP_COL // LANES)], axis=1).astype(BF16)
    m = m + jax.nn.sigmoid(proj(1)) * jnp.dot(yb, pb_ref[...], preferred_element_type=F32)

    mixed = jnp.dot(m.astype(BF16), wo_ref[...], preferred_element_type=F32)
    o_ref[...] = _layer_norm(ALPHA * x + mixed, ln1g_ref[...], ln1b_ref[...])


def _mix_call(x2d, y_attn, w_rest, conv_w, lng, lnb, ws, bsb, pa, pb, pc, wo, ln1g, ln1b, seq):
    t = x2d.shape[0]
    tiles_per_batch = seq // TILE
    weights = (w_rest, conv_w, lng, lnb, ws, bsb, pa, pb, pc, wo, ln1g, ln1b)
    return pl.pallas_call(
        functools.partial(_mix_kernel, tiles_per_batch=tiles_per_batch),
        out_shape=jax.ShapeDtypeStruct((t, D_MODEL), F32),
        grid=(t // TILE,),
        in_specs=[pl.BlockSpec((TILE, D_MODEL), lambda i: (i, 0)),
                  pl.BlockSpec((1, N_RES, ROWS_PER_RES, GROUP_COL),
                               lambda i: (i // tiles_per_batch, 0, i % tiles_per_batch, 0))]
                 + [_resident(w.shape) for w in weights],
        out_specs=pl.BlockSpec((TILE, D_MODEL), lambda i: (i, 0)),
        scratch_shapes=[pltpu.VMEM((SUBLANES, D_MODEL), F32),
                        pltpu.VMEM((GROUP_COL // LANES, TILE, LANES), F32)],
        compiler_params=pltpu.CompilerParams(dimension_semantics=("arbitrary",), vmem_limit_bytes=VMEM_LIMIT),
        name="mixers_out_ln",
    )(x2d, y_attn, *weights)


def _ffn_kernel(x_ref, wg_ref, wu_ref, wd_ref, g_ref, b_ref, o_ref):
    x = x_ref[...]
    xb = x.astype(BF16)
    gate = jnp.dot(xb, wg_ref[...], preferred_element_type=F32)
    up = jnp.dot(xb, wu_ref[...], preferred_element_type=F32)
    h = (jax.nn.silu(gate) * up).astype(BF16)
    down = jnp.dot(h, wd_ref[...], preferred_element_type=F32)
    o_ref[...] = _layer_norm(ALPHA * x + down, g_ref[...], b_ref[...])


def _ffn_call(x2d, wg, wu, wd, g, b):
    t = x2d.shape[0]
    weights = (wg, wu, wd, g, b)
    return pl.pallas_call(
        _ffn_kernel,
        out_shape=jax.ShapeDtypeStruct((t, D_MODEL), F32),
        grid=(t // TILE,),
        in_specs=[pl.BlockSpec((TILE, D_MODEL), lambda i: (i, 0))] + [_resident(w.shape) for w in weights],
        out_specs=pl.BlockSpec((TILE, D_MODEL), lambda i: (i, 0)),
        compiler_params=pltpu.CompilerParams(dimension_semantics=("arbitrary",), vmem_limit_bytes=VMEM_LIMIT),
        name="swiglu_ln",
    )(x2d, *weights)


def _band_masks():
    u = np.arange(ATTN_BLK)
    stored = (
        (u % N_RES) * SUBLANES + u // N_RES,
        (u % 4) * 32 + u // 4,
        u,
    )
    out = np.empty((N_GROUPS, ATTN_BLK, 2 * ATTN_BLK), np.float32)
    for g, st in enumerate(stored):
        orig = np.empty(ATTN_BLK, np.int64)
        orig[st] = u
        qi = orig[:, None]
        ki = orig[None, :]
        out[g, :, :ATTN_BLK] = np.where(ki >= qi, 0.0, NEG)
        out[g, :, ATTN_BLK:] = np.where(ki <= qi, 0.0, NEG)
    return out


def _rope_column_perm():
    idx = []
    for quad in range(2):
        for part in range(2):
            for h in range(4 * quad, 4 * quad + 4):
                idx.extend(range(HEAD_DIM * h + HALF * part, HEAD_DIM * h + HALF * (part + 1)))
    return np.asarray(idx, np.int32)


def kernel(x, positions, w_in, conv_w, gmlp_ln_g, gmlp_ln_b, w_s, b_s, p_a, p_b, p_c, w_o, ln1_g, ln1_b, w_gate, w_up,
           w_down, ln2_g, ln2_b):
    batch, seq, d = x.shape
    assert d == D_MODEL and seq % CHUNK == 0 and w_in.shape[0] == DEPTH
    t = batch * seq
    depth = w_in.shape[0]

    pos = positions.reshape(batch, seq // TILE, ROWS_PER_RES, N_RES).transpose(0, 1, 3, 2).reshape(t, 1).astype(F32)
    inv_freq = ROPE_THETA ** (-jnp.arange(HALF, dtype=F32) / HALF)
    cos_t, sin_t = _rope_tables(pos, jnp.tile(inv_freq, LANES // HALF).reshape(1, LANES))

    perm = _rope_column_perm()
    masks = jnp.asarray(_band_masks())
    row = lambda v: v.reshape(1, -1)

    h = x.reshape(t, D_MODEL)
    for l in range(depth):
        w = w_in[l]
        qkv_cols = []
        for g in range(N_GROUPS):
            base = _OFF_ATTN + 3 * g * GROUP_COL
            qkv_cols += [w[:, base:base + GROUP_COL][:, perm],
                         w[:, base + GROUP_COL:base + 2 * GROUP_COL][:, perm],
                         w[:, base + 2 * GROUP_COL:base + 3 * GROUP_COL]]
        w_qkv = jnp.concatenate(qkv_cols, axis=1).astype(BF16)
        w_rest = jnp.concatenate([w[:, :_OFF_ATTN], w[:, _OFF_GMLP:]], axis=1).astype(BF16)

        qkv = _qkv_call(h, w_qkv, cos_t, sin_t)
        y_attn = _attn_call(qkv, masks, batch, seq)
        h = _mix_call(h, y_attn, w_rest, conv_w[l], row(gmlp_ln_g[l]), row(gmlp_ln_b[l]), w_s[l].astype(BF16),
                      jnp.repeat(b_s[l].T, GMLP_CHUNK, axis=1), p_a[l].astype(BF16), p_b[l].astype(BF16),
                      p_c[l].astype(BF16), w_o[l].astype(BF16), row(ln1_g[l]), row(ln1_b[l]), seq)
        h = _ffn_call(h, w_gate[l].astype(BF16), w_up[l].astype(BF16), w_down[l].astype(BF16),
                      row(ln2_g[l]), row(ln2_b[l]))
    return h.reshape(batch, seq, D_MODEL)
```

```python
import functools

import numpy as np
import jax
import jax.numpy as jnp
from jax import lax
from jax.experimental import pallas as pl
from jax.experimental.pallas import tpu as pltpu

F32 = jnp.float32
BF16 = jnp.bfloat16

D_MODEL = 1024
HEADS = 8
HEAD_DIM = 64
HALF = HEAD_DIM // 2
GROUP_COL = HEADS * HEAD_DIM
DILATIONS = (1, 4, 16)
N_GROUPS = len(DILATIONS)
ATTN_BLK = 128
N_RES = 16
CHUNK = N_RES * ATTN_BLK
GMLP_GROUPS = 8
GMLP_CHUNK = 128
D_FF = 2816
DEPTH = 2
ALPHA = (2 * DEPTH) ** 0.25
LN_EPS = 1e-5
ROPE_THETA = 10000.0

TILE = 512
ROWS_PER_RES = TILE // N_RES
LANES = 128
SUBLANES = 8
NEG = -1e30
VMEM_LIMIT = 56 * 1024 * 1024

_OFF_ATTN = 6 * D_MODEL
_QKV_COLS = 3 * GROUP_COL
_OFF_GMLP = _OFF_ATTN + N_GROUPS * _QKV_COLS
_REST_COLS = _OFF_ATTN + 2 * D_MODEL
_FFN_DOWN_BLOCKS = 16

_P_CONV, _P_GLN_G, _P_GLN_B, _P_LN1_G, _P_LN1_B, _P_LN2_G, _P_LN2_B = 0, 3, 4, 5, 6, 7, 8
_P_ROWS = 9


def _resident(shape):
    nd = len(shape)
    return pl.BlockSpec(shape, lambda *_: (0,) * nd, pipeline_mode=pl.Buffered(1))


def _layer_resident(shape, layer):
    nd = len(shape)
    return pl.BlockSpec((None,) + tuple(shape[1:]), lambda *_: (layer,) + (0,) * (nd - 1),
                        pipeline_mode=pl.Buffered(1))


def _layer_norm(v, g, b):
    mu = jnp.mean(v, axis=-1, keepdims=True)
    c = v - mu
    var = jnp.mean(c * c, axis=-1, keepdims=True)
    return c * lax.rsqrt(var + LN_EPS) * g + b


def _gelu(v):
    return 0.5 * v * (1.0 + lax.erf(v * 0.7071067811865476))


class _CastJob:
    def __init__(self, src, layer, rows, cols, n_blocks, *, col_block=0, by_cols=False, col_map=None):
        self.src, self.n_blocks = src, n_blocks
        last = n_blocks - 1
        if by_cols:
            bw = cols // n_blocks
            self.in_spec = pl.BlockSpec((None, rows, bw), lambda i: (layer, 0, col_map(jnp.minimum(i, last))))
            self.out_spec = pl.BlockSpec((rows, bw), lambda i: (0, jnp.minimum(i, last)))
        else:
            rb = rows // n_blocks
            self.in_spec = pl.BlockSpec((None, rb, cols), lambda i: (layer, jnp.minimum(i, last), col_block))
            self.out_spec = pl.BlockSpec((rb, cols), lambda i: (jnp.minimum(i, last), 0))
        self.out_shape = jax.ShapeDtypeStruct((rows, cols), BF16)


def _run_casts(jobs, src_refs, dst_refs, step, n_steps):
    for job, s, d in zip(jobs, src_refs, dst_refs):
        if job.n_blocks >= n_steps:
            d[...] = s[...].astype(BF16)
        else:
            @pl.when(step < job.n_blocks)
            def _(s=s, d=d):
                d[...] = s[...].astype(BF16)


def _rope_table_kernel(pos_ref, inv_ref, cos_ref, sin_ref):
    ang = pos_ref[...] * inv_ref[...]
    first_half = (lax.broadcasted_iota(jnp.int32, (1, LANES), 1) % HEAD_DIM) < HALF
    cos_ref[...] = jnp.cos(ang)
    sin_ref[...] = jnp.where(first_half, -jnp.sin(ang), jnp.sin(ang))


def _rope_tables(pos_col, inv_row):
    t = pos_col.shape[0]
    rows = 2048
    return pl.pallas_call(
        _rope_table_kernel,
        out_shape=(jax.ShapeDtypeStruct((t, LANES), F32),) * 2,
        grid=(t // rows,),
        in_specs=[pl.BlockSpec((rows, 1), lambda i: (i, 0)), _resident((1, LANES))],
        out_specs=(pl.BlockSpec((rows, LANES), lambda i: (i, 0)),) * 2,
        compiler_params=pltpu.CompilerParams(dimension_semantics=("arbitrary",)),
        name="rope_tables",
    )(pos_col, inv_row)


def _qkv_kernel(*refs, n_casts, n_steps, jobs):
    x_refs = refs[:8]
    w_refs = refs[8:11]
    cos_ref, sin_ref = refs[11:13]
    cast_src = refs[13:13 + n_casts]
    out_refs = refs[13 + n_casts:22 + n_casts]
    cast_dst = refs[22 + n_casts:]
    _run_casts(jobs, cast_src, cast_dst, pl.program_id(0), n_steps)

    rows = [jnp.concatenate([xr[pl.ds(rho, ROWS_PER_RES, stride=N_RES), :] for xr in x_refs], axis=1)
            for rho in range(N_RES)]
    xp = jnp.concatenate(rows, axis=0).astype(BF16)
    cos = cos_ref[...]
    sin = sin_ref[...]
    scale = HEAD_DIM ** -0.5
    tables = ((cos * scale, sin * scale), (cos, sin))
    first_half = (lax.broadcasted_iota(jnp.int32, (TILE, LANES), 1) % HEAD_DIM) < HALF

    for g in range(N_GROUPS):
        for j in range(3):
            y = jnp.dot(xp, w_refs[g][:, GROUP_COL * j:GROUP_COL * (j + 1)], preferred_element_type=F32)
            if j < 2:
                c, s = tables[j]
                parts = []
                for sl in range(GROUP_COL // LANES):
                    a = y[:, LANES * sl:LANES * (sl + 1)]
                    partner = jnp.where(first_half, pltpu.roll(a, LANES - HALF, 1), pltpu.roll(a, HALF, 1))
                    parts.append(a * c + partner * s)
                y = jnp.concatenate(parts, axis=1)
            o_ref = out_refs[3 * g + j]
            if g == 0:
                pieces = [y[32 * rho + 8 * c:32 * rho + 8 * c + 8, :] for c in range(4) for rho in range(N_RES)]
                o_ref[...] = jnp.concatenate(pieces, axis=0).astype(BF16)
            elif g == 1:
                for a in range(4):
                    slab = jnp.concatenate([y[32 * (4 * b + a):32 * (4 * b + a) + 32, :] for b in range(4)], axis=0)
                    o_ref[:, GROUP_COL * a:GROUP_COL * (a + 1)] = slab.astype(BF16)
            else:
                for rho in range(N_RES):
                    o_ref[:, GROUP_COL * rho:GROUP_COL * (rho + 1)] = y[32 * rho:32 * rho + 32, :].astype(BF16)


def _qkv_call(x2d, w_groups, cos_t, sin_t, jobs):
    t = x2d.shape[0]
    nt = t // TILE
    x_specs = [pl.BlockSpec((TILE, LANES), functools.partial(lambda i, c: (i, c), c=c)) for c in range(8)]
    out_shapes, out_specs = [], []
    for d in DILATIONS:
        for _ in range(3):
            out_shapes.append(jax.ShapeDtypeStruct((t // d, d * GROUP_COL), BF16))
            out_specs.append(pl.BlockSpec((TILE // d, d * GROUP_COL), lambda i: (i, 0)))
    outs = pl.pallas_call(
        functools.partial(_qkv_kernel, n_casts=len(jobs), n_steps=nt, jobs=jobs),
        out_shape=tuple(out_shapes) + tuple(j.out_shape for j in jobs),
        grid=(nt,),
        in_specs=x_specs + [_resident(w.shape) for w in w_groups]
                 + [pl.BlockSpec((TILE, LANES), lambda i: (i, 0))] * 2 + [j.in_spec for j in jobs],
        out_specs=tuple(out_specs) + tuple(j.out_spec for j in jobs),
        compiler_params=pltpu.CompilerParams(dimension_semantics=("arbitrary",), vmem_limit_bytes=VMEM_LIMIT),
        name="qkv_rope",
    )(*([x2d] * 8), *w_groups, cos_t, sin_t, *[j.src for j in jobs])
    return outs[:9], outs[9:]


def _mask16(cond):
    return jnp.where(cond, 1.0, 0.0).astype(BF16) > 0


def _value_rhs(v, lo):
    zero = jnp.zeros_like(v)
    one = jnp.ones_like(v)
    top = jnp.concatenate([jnp.where(lo, v, zero), jnp.where(lo, one, zero)], axis=1)
    bot = jnp.concatenate([jnp.where(lo, zero, v), jnp.where(lo, zero, one)], axis=1)
    return jnp.concatenate([top, bot], axis=0)


def _attn_block(q, kp, kc, vp, vc, bias_p, bias_c, state_get, state_put, emit):
    lo16 = _mask16(lax.broadcasted_iota(jnp.int32, (ATTN_BLK, LANES), 1) < HEAD_DIM)
    lo = lax.broadcasted_iota(jnp.int32, (1, LANES), 1) < HEAD_DIM
    for j in range(HEADS // 2):
        sl = slice(LANES * j, LANES * (j + 1))
        qj = q[:, sl]
        zero = jnp.zeros_like(qj)
        lhs = jnp.concatenate([jnp.where(lo16, qj, zero), jnp.where(lo16, zero, qj)], axis=0)
        keys = jnp.concatenate([kp[:, sl], kc[:, sl]], axis=0)
        scores = lax.dot_general(lhs, keys, (((1,), (1,)), ((), ())), preferred_element_type=F32)
        probs, maxes = [], []
        for hl in range(2):
            sp = scores[ATTN_BLK * hl:ATTN_BLK * (hl + 1), :ATTN_BLK] + bias_p
            sc = scores[ATTN_BLK * hl:ATTN_BLK * (hl + 1), ATTN_BLK:] + bias_c
            mb = jnp.max(jnp.maximum(sp, sc), axis=1, keepdims=True)
            probs.append((jnp.exp(sp - mb).astype(BF16), jnp.exp(sc - mb).astype(BF16)))
            maxes.append(mb)
        lhs_pv = jnp.concatenate([probs[0][0], probs[1][0], probs[0][1], probs[1][1]], axis=1)
        rhs = jnp.concatenate([_value_rhs(vp[:, sl], lo16), _value_rhs(vc[:, sl], lo16)], axis=0)
        ol = jnp.dot(lhs_pv, rhs, preferred_element_type=F32)
        o_new, l_new = ol[:, :LANES], ol[:, LANES:]
        m_new = jnp.where(lo, maxes[0], maxes[1])
        old = state_get(j)
        if old is not None:
            acc0, l0, m0 = old
            m = jnp.maximum(m0, m_new)
            a0 = jnp.exp(m0 - m)
            a1 = jnp.exp(m_new - m)
            o_new = acc0 * a0 + o_new * a1
            l_new = l0 * a0 + l_new * a1
            m_new = m
        if emit is not None:
            emit(j, o_new / l_new)
        else:
            state_put(j, o_new, l_new, m_new)


def _attn_kernel(q1, k1, v1, kp1, vp1, q2, k2, v2, kp2, vp2, q3, k3, v3, kp3, vp3, mask_ref, out_ref,
                 acc_s, l_s, m_s):
    n = pl.program_id(1)
    s = pl.program_id(2)
    state = (acc_s, l_s, m_s)

    def lanes(j):
        return slice(LANES * j, LANES * (j + 1))

    @pl.when(s < 4)
    def _group1():
        for c in range(4):
            k_blk = pl.multiple_of((4 * s + c) * SUBLANES, SUBLANES)

            def put(j, *vals, k_blk=k_blk):
                for ref, val in zip(state, vals):
                    for rho in range(N_RES):
                        ref[rho, pl.ds(k_blk, SUBLANES), lanes(j)] = val[SUBLANES * rho:SUBLANES * (rho + 1), :]

            rows = slice(ATTN_BLK * c, ATTN_BLK * (c + 1))
            prev = slice(ATTN_BLK * (c - 1), ATTN_BLK * c)
            bias_p = mask_ref[0, :, :ATTN_BLK]
            if c == 0:
                bias_p = jnp.where(jnp.logical_and(n == 0, s == 0), NEG, bias_p)
            _attn_block(q1[rows, :], kp1[...] if c == 0 else k1[prev, :], k1[rows, :],
                        vp1[...] if c == 0 else v1[prev, :], v1[rows, :],
                        bias_p, mask_ref[0, :, ATTN_BLK:], lambda j: None, put, None)

    @pl.when(jnp.logical_and(s >= 4, s < 8))
    def _group2():
        t = s - 4
        base = pl.multiple_of(t * 32, 32)
        first = jnp.logical_and(n == 0, t == 0)
        for a in range(4):
            def get(j, a=a):
                return tuple(jnp.concatenate([ref[4 * b + a, pl.ds(base, 32), lanes(j)] for b in range(4)], axis=0)
                             for ref in state)

            def put(j, *vals, a=a):
                for ref, val in zip(state, vals):
                    for b in range(4):
                        ref[4 * b + a, pl.ds(base, 32), lanes(j)] = val[32 * b:32 * (b + 1), :]

            cols = slice(GROUP_COL * a, GROUP_COL * (a + 1))
            bias_p = jnp.where(first, NEG, mask_ref[1, :, :ATTN_BLK])
            _attn_block(q2[:, cols], kp2[:, cols], k2[:, cols], vp2[:, cols], v2[:, cols],
                        bias_p, mask_ref[1, :, ATTN_BLK:], get, put, None)

    @pl.when(s >= 8)
    def _group3():
        t = s - 8
        for c in range(4):
            rho = 4 * t + c

            def get(j, rho=rho):
                return tuple(ref[rho, :, lanes(j)] for ref in state)

            def emit(j, y, c=c):
                out_ref[0, c, :, lanes(j)] = y.astype(BF16)

            cols = slice(GROUP_COL * c, GROUP_COL * (c + 1))
            bias_p = jnp.where(n == 0, NEG, mask_ref[2, :, :ATTN_BLK])
            _attn_block(q3[:, cols], kp3[:, cols], k3[:, cols], vp3[:, cols], v3[:, cols],
                        bias_p, mask_ref[2, :, ATTN_BLK:], get, None, emit)


def _attn_call(qkv, masks, batch, seq):
    nchunk = seq // CHUNK
    tiles = seq // TILE

    def step(s, g):
        return jnp.clip(s - 4 * g, 0, 3)

    def cur1(b, n, s):
        return (b * tiles + 4 * n + step(s, 0), 0)

    def prev1(b, n, s):
        return (jnp.maximum(4 * (b * tiles + 4 * n + step(s, 0)) - 1, 0), 0)

    def cur2(b, n, s):
        return (b * tiles + 4 * n + step(s, 1), 0)

    def prev2(b, n, s):
        return (jnp.maximum(b * tiles + 4 * n + step(s, 1) - 1, 0), 0)

    def cur3(b, n, s):
        return (b * nchunk + n, step(s, 2))

    def prev3(b, n, s):
        return (jnp.maximum(b * nchunk + n - 1, 0), step(s, 2))

    big = (ATTN_BLK, 4 * GROUP_COL)
    in_specs = [
        pl.BlockSpec((TILE, GROUP_COL), cur1), pl.BlockSpec((TILE, GROUP_COL), cur1), pl.BlockSpec((TILE, GROUP_COL), cur1),
        pl.BlockSpec((ATTN_BLK, GROUP_COL), prev1), pl.BlockSpec((ATTN_BLK, GROUP_COL), prev1),
        pl.BlockSpec(big, cur2), pl.BlockSpec(big, cur2), pl.BlockSpec(big, cur2),
        pl.BlockSpec(big, prev2), pl.BlockSpec(big, prev2),
        pl.BlockSpec(big, cur3), pl.BlockSpec(big, cur3), pl.BlockSpec(big, cur3),
        pl.BlockSpec(big, prev3), pl.BlockSpec(big, prev3),
        _resident(masks.shape),
    ]
    q1, k1, v1, q2, k2, v2, q3, k3, v3 = qkv
    return pl.pallas_call(
        _attn_kernel,
        out_shape=jax.ShapeDtypeStruct((batch, N_RES, seq // N_RES, GROUP_COL), BF16),
        grid=(batch, nchunk, 12),
        in_specs=in_specs,
        out_specs=pl.BlockSpec((1, 4, ATTN_BLK, GROUP_COL), lambda b, n, s: (b, step(s, 2), n, 0)),
        scratch_shapes=[pltpu.VMEM((N_RES, ATTN_BLK, GROUP_COL), F32)] * 3,
        compiler_params=pltpu.CompilerParams(dimension_semantics=("arbitrary",) * 3, vmem_limit_bytes=VMEM_LIMIT),
        name="dilated_attention",
    )(q1, k1, v1, k1, v1, q2, k2, v2, k2, v2, q3, k3, v3, k3, v3, masks)


def _mix_kernel(*refs, tiles_per_batch, n_casts, n_steps, jobs):
    x_ref, y_ref, w_ref, vec_ref, ws_ref, bsb_ref, pa_ref, pb_ref, pc_ref, wo_ref = refs[:10]
    cast_src = refs[10:10 + n_casts]
    o_ref = refs[10 + n_casts]
    cast_dst = refs[11 + n_casts:11 + 2 * n_casts]
    zc_ref, ys_ref = refs[11 + 2 * n_casts:]
    i = pl.program_id(0)
    _run_casts(jobs, cast_src, cast_dst, i, n_steps)

    x = x_ref[...]
    xb = x.astype(BF16)

    def vec(r):
        return vec_ref[r:r + 1, :]

    def proj(k):
        return jnp.dot(xb, w_ref[:, D_MODEL * k:D_MODEL * (k + 1)], preferred_element_type=F32)

    @pl.when(i % tiles_per_batch == 0)
    def _():
        zc_ref[...] = jnp.zeros_like(zc_ref)

    z = proj(4) * proj(5)
    head = jnp.concatenate([zc_ref[...], z[:SUBLANES, :]], axis=0)
    zc_ref[...] = z[TILE - SUBLANES:, :]
    conv = vec(_P_CONV + 2) * z
    for shift in (1, 2):
        zs = jnp.concatenate([pltpu.roll(head, shift, 0)[SUBLANES:, :], pltpu.roll(z, shift, 0)[SUBLANES:, :]], axis=0)
        conv = conv + vec(_P_CONV + 2 - shift) * zs
    ya = (proj(3) * conv).astype(BF16)
    m = jax.nn.sigmoid(proj(0)) * jnp.dot(ya, pa_ref[...], preferred_element_type=F32)

    u = _gelu(proj(6))
    vb = _layer_norm(_gelu(proj(7)), vec(_P_GLN_G), vec(_P_GLN_B)).astype(BF16)
    n_chunks = TILE // GMLP_CHUNK
    tril = (lax.broadcasted_iota(jnp.int32, (GMLP_CHUNK, GMLP_CHUNK), 0)
            >= lax.broadcasted_iota(jnp.int32, (GMLP_CHUNK, GMLP_CHUNK), 1))
    cols = []
    for g in range(GMLP_GROUPS):
        gl = slice(LANES * g, LANES * (g + 1))
        wg = jnp.where(tril, ws_ref[g], 0.0).astype(BF16)
        rhs = jnp.concatenate([vb[GMLP_CHUNK * c:GMLP_CHUNK * (c + 1), gl] for c in range(n_chunks)], axis=1)
        sp = jnp.dot(wg, rhs, preferred_element_type=F32)
        bias = bsb_ref[:, gl]
        cols.append(jnp.concatenate([sp[:, LANES * c:LANES * (c + 1)] + bias for c in range(n_chunks)], axis=0))
    yc = (u * jnp.concatenate(cols, axis=1)).astype(BF16)
    m = m + jax.nn.sigmoid(proj(2)) * jnp.dot(yc, pc_ref[...], preferred_element_type=F32)

    for rho in range(N_RES):
        piece = y_ref[0, rho].astype(F32)
        for sl in range(GROUP_COL // LANES):
            ys_ref[sl, pl.ds(rho, ROWS_PER_RES, stride=N_RES), :] = piece[:, LANES * sl:LANES * (sl + 1)]
    yb = jnp.concatenate([ys_ref[sl] for sl in range(GROUP_COL // LANES)], axis=1).astype(BF16)
    m = m + jax.nn.sigmoid(proj(1)) * jnp.dot(yb, pb_ref[...], preferred_element_type=F32)

    mixed = jnp.dot(m.astype(BF16), wo_ref[...], preferred_element_type=F32)
    o_ref[...] = _layer_norm(ALPHA * x + mixed, vec(_P_LN1_G), vec(_P_LN1_B))


def _mix_call(x2d, y_attn, w_rest, vecs, w_s, bsb, pa, pb, pc, wo, layer, seq, jobs):
    t = x2d.shape[0]
    nt = t // TILE
    tiles_per_batch = seq // TILE
    outs = pl.pallas_call(
        functools.partial(_mix_kernel, tiles_per_batch=tiles_per_batch, n_casts=len(jobs), n_steps=nt, jobs=jobs),
        out_shape=(jax.ShapeDtypeStruct((t, D_MODEL), F32),) + tuple(j.out_shape for j in jobs),
        grid=(nt,),
        in_specs=[pl.BlockSpec((TILE, D_MODEL), lambda i: (i, 0)),
                  pl.BlockSpec((1, N_RES, ROWS_PER_RES, GROUP_COL),
                               lambda i: (i // tiles_per_batch, 0, i % tiles_per_batch, 0)),
                  _resident(w_rest.shape), _layer_resident(vecs.shape, layer), _layer_resident(w_s.shape, layer),
                  _layer_resident(bsb.shape, layer)]
                 + [_resident(w.shape) for w in (pa, pb, pc, wo)] + [j.in_spec for j in jobs],
        out_specs=(pl.BlockSpec((TILE, D_MODEL), lambda i: (i, 0)),) + tuple(j.out_spec for j in jobs),
        scratch_shapes=[pltpu.VMEM((SUBLANES, D_MODEL), F32),
                        pltpu.VMEM((GROUP_COL // LANES, TILE, LANES), F32)],
        compiler_params=pltpu.CompilerParams(dimension_semantics=("arbitrary",), vmem_limit_bytes=VMEM_LIMIT),
        name="mixers_out_ln",
    )(x2d, y_attn, w_rest, vecs, w_s, bsb, pa, pb, pc, wo, *[j.src for j in jobs])
    return outs[0], outs[1:]


def _ffn_kernel(*refs, n_casts, n_steps, jobs):
    x_ref, wg_ref, wu_ref, wd_ref, vec_ref = refs[:5]
    cast_src = refs[5:5 + n_casts]
    o_ref = refs[5 + n_casts]
    cast_dst = refs[6 + n_casts:]
    _run_casts(jobs, cast_src, cast_dst, pl.program_id(0), n_steps)

    x = x_ref[...]
    xb = x.astype(BF16)
    gate = jnp.dot(xb, wg_ref[...], preferred_element_type=F32)
    up = jnp.dot(xb, wu_ref[...], preferred_element_type=F32)
    h = (jax.nn.silu(gate) * up).astype(BF16)
    down = jnp.dot(h, wd_ref[...], preferred_element_type=F32)
    o_ref[...] = _layer_norm(ALPHA * x + down, vec_ref[_P_LN2_G:_P_LN2_G + 1, :], vec_ref[_P_LN2_B:_P_LN2_B + 1, :])


def _ffn_call(x2d, wg, wu, wd, vecs, layer, jobs):
    t = x2d.shape[0]
    nt = t // TILE
    outs = pl.pallas_call(
        functools.partial(_ffn_kernel, n_casts=len(jobs), n_steps=nt, jobs=jobs),
        out_shape=(jax.ShapeDtypeStruct((t, D_MODEL), F32),) + tuple(j.out_shape for j in jobs),
        grid=(nt,),
        in_specs=[pl.BlockSpec((TILE, D_MODEL), lambda i: (i, 0))] + [_resident(w.shape) for w in (wg, wu, wd)]
                 + [_layer_resident(vecs.shape, layer)] + [j.in_spec for j in jobs],
        out_specs=(pl.BlockSpec((TILE, D_MODEL), lambda i: (i, 0)),) + tuple(j.out_spec for j in jobs),
        compiler_params=pltpu.CompilerParams(dimension_semantics=("arbitrary",), vmem_limit_bytes=VMEM_LIMIT),
        name="swiglu_ln",
    )(x2d, wg, wu, wd, vecs, *[j.src for j in jobs])
    return outs[0], outs[1:]


def _band_masks():
    u = np.arange(ATTN_BLK)
    stored = (
        (u % N_RES) * SUBLANES + u // N_RES,
        (u % 4) * 32 + u // 4,
        u,
    )
    out = np.empty((N_GROUPS, ATTN_BLK, 2 * ATTN_BLK), np.float32)
    for g, st in enumerate(stored):
        orig = np.empty(ATTN_BLK, np.int64)
        orig[st] = u
        qi = orig[:, None]
        ki = orig[None, :]
        out[g, :, :ATTN_BLK] = np.where(ki >= qi, 0.0, NEG)
        out[g, :, ATTN_BLK:] = np.where(ki <= qi, 0.0, NEG)
    return out


def kernel(x, positions, w_in, conv_w, gmlp_ln_g, gmlp_ln_b, w_s, b_s, p_a, p_b, p_c, w_o, ln1_g, ln1_b, w_gate, w_up,
           w_down, ln2_g, ln2_b):
    batch, seq, d = x.shape
    t = batch * seq
    nt = t // TILE
    depth = w_in.shape[0]
    assert d == D_MODEL and seq % CHUNK == 0 and depth == DEPTH and nt % 16 == 0
    n_cast = 32 if nt >= 32 else 16

    pos = positions.reshape(batch, seq // TILE, ROWS_PER_RES, N_RES).transpose(0, 1, 3, 2).reshape(t, 1).astype(F32)
    inv_freq = ROPE_THETA ** (-jnp.arange(HALF, dtype=F32) / HALF)
    cos_t, sin_t = _rope_tables(pos, jnp.tile(inv_freq, LANES // HALF).reshape(1, LANES))

    masks = jnp.asarray(_band_masks())
    vecs = jnp.concatenate([conv_w] + [v[:, None, :] for v in (gmlp_ln_g, gmlp_ln_b, ln1_g, ln1_b, ln2_g, ln2_b)],
                           axis=1)
    bsb = jnp.repeat(jnp.swapaxes(b_s, 1, 2), GMLP_CHUNK, axis=2)

    rest_bw = _REST_COLS // n_cast
    gap = (_OFF_GMLP - _OFF_ATTN) // rest_bw

    def rest_col_block(i):
        return jnp.where(i < _OFF_ATTN // rest_bw, i, i + gap)

    def qkv_jobs(layer):
        return [_CastJob(w_in, layer, D_MODEL, _QKV_COLS, n_cast, col_block=_OFF_ATTN // _QKV_COLS + g)
                for g in range(N_GROUPS)]

    w_groups = tuple(w_in[0, :, _OFF_ATTN + _QKV_COLS * g:_OFF_ATTN + _QKV_COLS * (g + 1)].astype(BF16)
                     for g in range(N_GROUPS))
    h = x.reshape(t, D_MODEL)
    for l in range(depth):
        jobs = [_CastJob(w_in, l, D_MODEL, _REST_COLS, n_cast, by_cols=True, col_map=rest_col_block),
                _CastJob(p_a, l, D_MODEL, D_MODEL, n_cast), _CastJob(p_b, l, GROUP_COL, D_MODEL, n_cast),
                _CastJob(p_c, l, D_MODEL, D_MODEL, n_cast), _CastJob(w_o, l, D_MODEL, D_MODEL, n_cast)]
        qkv, (w_rest, pa, pb, pc, wo) = _qkv_call(h, w_groups, cos_t, sin_t, jobs)
        y_attn = _attn_call(qkv, masks, batch, seq)
        jobs = [_CastJob(w_gate, l, D_MODEL, D_FF, n_cast), _CastJob(w_up, l, D_MODEL, D_FF, n_cast),
                _CastJob(w_down, l, D_FF, D_MODEL, _FFN_DOWN_BLOCKS)]
        h, (wg, wu, wd) = _mix_call(h, y_attn, w_rest, vecs, w_s, bsb, pa, pb, pc, wo, l, seq, jobs)
        jobs = qkv_jobs(l + 1) if l + 1 < depth else []
        h, w_next = _ffn_call(h, wg, wu, wd, vecs, l, jobs)
        w_groups = tuple(w_next)
    return h.reshape(batch, seq, D_MODEL)
```

```python
import functools

import numpy as np
import jax
import jax.numpy as jnp
from jax import lax
from jax.experimental import pallas as pl
from jax.experimental.pallas import tpu as pltpu

F32 = jnp.float32
BF16 = jnp.bfloat16

D_MODEL = 1024
HEADS = 8
HEAD_DIM = 64
HALF = HEAD_DIM // 2
GROUP_COL = HEADS * HEAD_DIM
DILATIONS = (1, 4, 16)
N_GROUPS = len(DILATIONS)
ATTN_BLK = 128
N_RES = 16
CHUNK = N_RES * ATTN_BLK
GMLP_GROUPS = 8
GMLP_CHUNK = 128
D_FF = 2816
DEPTH = 2
ALPHA = (2 * DEPTH) ** 0.25
LN_EPS = 1e-5
ROPE_THETA = 10000.0

TILE = 512
ROWS_PER_RES = TILE // N_RES
LANES = 128
SUBLANES = 8
NEG = -1e30
VMEM_LIMIT = 56 * 1024 * 1024

_OFF_ATTN = 6 * D_MODEL
_QKV_COLS = 3 * GROUP_COL
_OFF_GMLP = _OFF_ATTN + N_GROUPS * _QKV_COLS
_REST_COLS = _OFF_ATTN + 2 * D_MODEL
_FFN_DOWN_BLOCKS = 16

_P_CONV, _P_GLN_G, _P_GLN_B, _P_LN1_G, _P_LN1_B, _P_LN2_G, _P_LN2_B = 0, 3, 4, 5, 6, 7, 8
_P_ROWS = 9


def _resident(shape):
    nd = len(shape)
    return pl.BlockSpec(shape, lambda *_: (0,) * nd, pipeline_mode=pl.Buffered(1))


def _layer_resident(shape, layer):
    nd = len(shape)
    return pl.BlockSpec((None,) + tuple(shape[1:]), lambda *_: (layer,) + (0,) * (nd - 1),
                        pipeline_mode=pl.Buffered(1))


def _layer_norm(v, g, b):
    mu = jnp.mean(v, axis=-1, keepdims=True)
    c = v - mu
    var = jnp.mean(c * c, axis=-1, keepdims=True)
    return c * lax.rsqrt(var + LN_EPS) * g + b


def _gelu(v):
    return 0.5 * v * (1.0 + lax.erf(v * 0.7071067811865476))


class _CastJob:
    def __init__(self, src, layer, rows, cols, n_blocks, *, col_block=0, by_cols=False, col_map=None):
        self.src, self.n_blocks = src, n_blocks
        last = n_blocks - 1
        if by_cols:
            bw = cols // n_blocks
            self.in_spec = pl.BlockSpec((None, rows, bw), lambda i: (layer, 0, col_map(jnp.minimum(i, last))))
            self.out_spec = pl.BlockSpec((rows, bw), lambda i: (0, jnp.minimum(i, last)))
        else:
            rb = rows // n_blocks
            self.in_spec = pl.BlockSpec((None, rb, cols), lambda i: (layer, jnp.minimum(i, last), col_block))
            self.out_spec = pl.BlockSpec((rb, cols), lambda i: (jnp.minimum(i, last), 0))
        self.out_shape = jax.ShapeDtypeStruct((rows, cols), BF16)


def _run_casts(jobs, src_refs, dst_refs, step, n_steps):
    for job, s, d in zip(jobs, src_refs, dst_refs):
        if job.n_blocks >= n_steps:
            d[...] = s[...].astype(BF16)
        else:
            @pl.when(step < job.n_blocks)
            def _(s=s, d=d):
                d[...] = s[...].astype(BF16)


def _cast_kernel(*refs):
    n = len(refs) // 2
    for s, d in zip(refs[:n], refs[n:]):
        d[...] = s[...].astype(BF16)


def _cast_call(jobs, n_steps):
    return pl.pallas_call(
        _cast_kernel,
        out_shape=tuple(j.out_shape for j in jobs),
        grid=(n_steps,),
        in_specs=[j.in_spec for j in jobs],
        out_specs=tuple(j.out_spec for j in jobs),
        compiler_params=pltpu.CompilerParams(dimension_semantics=("arbitrary",)),
        name="cast_weights",
    )(*[j.src for j in jobs])


def _rope_tables(pos4, inv_row):
    ang = pos4 * inv_row
    lane = lax.broadcasted_iota(jnp.int32, (TILE // 4, LANES), 1)
    group = lane // HALF
    sign = jnp.where((lane % HEAD_DIM) < HALF, -1.0, 1.0)
    out = []
    for dense, sgn in ((jnp.cos(ang), None), (jnp.sin(ang), sign)):
        rolled = [dense] + [pltpu.roll(dense, HALF * k, 1) for k in range(1, 4)]
        slabs = []
        for a in range(4):
            e = rolled[(0 - a) % 4]
            for g in range(1, 4):
                e = jnp.where(group == g, rolled[(g - a) % 4], e)
            slabs.append(e if sgn is None else e * sgn)
        out.append(jnp.concatenate(slabs, axis=0))
    return out


def _qkv_kernel(*refs, n_casts, n_steps, jobs):
    x_refs = refs[:8]
    w_refs = refs[8:11]
    pos_ref, inv_ref = refs[11:13]
    cast_src = refs[13:13 + n_casts]
    out_refs = refs[13 + n_casts:22 + n_casts]
    cast_dst = refs[22 + n_casts:]
    _run_casts(jobs, cast_src, cast_dst, pl.program_id(0), n_steps)

    rows = [jnp.concatenate([xr[pl.ds(rho, ROWS_PER_RES, stride=N_RES), :] for xr in x_refs], axis=1)
            for rho in range(N_RES)]
    xp = jnp.concatenate(rows, axis=0).astype(BF16)
    cos, sin = _rope_tables(pos_ref[...], inv_ref[...])
    scale = HEAD_DIM ** -0.5
    tables = ((cos * scale, sin * scale), (cos, sin))
    first_half = (lax.broadcasted_iota(jnp.int32, (TILE, LANES), 1) % HEAD_DIM) < HALF

    for g in range(N_GROUPS):
        for j in range(3):
            y = jnp.dot(xp, w_refs[g][:, GROUP_COL * j:GROUP_COL * (j + 1)], preferred_element_type=F32)
            if j < 2:
                c, s = tables[j]
                parts = []
                for sl in range(GROUP_COL // LANES):
                    a = y[:, LANES * sl:LANES * (sl + 1)]
                    partner = jnp.where(first_half, pltpu.roll(a, LANES - HALF, 1), pltpu.roll(a, HALF, 1))
                    parts.append(a * c + partner * s)
                y = jnp.concatenate(parts, axis=1)
            o_ref = out_refs[3 * g + j]
            if g == 0:
                pieces = [y[32 * rho + 8 * c:32 * rho + 8 * c + 8, :] for c in range(4) for rho in range(N_RES)]
                o_ref[...] = jnp.concatenate(pieces, axis=0).astype(BF16)
            elif g == 1:
                for a in range(4):
                    slab = jnp.concatenate([y[32 * (4 * b + a):32 * (4 * b + a) + 32, :] for b in range(4)], axis=0)
                    o_ref[:, GROUP_COL * a:GROUP_COL * (a + 1)] = slab.astype(BF16)
            else:
                for rho in range(N_RES):
                    o_ref[:, GROUP_COL * rho:GROUP_COL * (rho + 1)] = y[32 * rho:32 * rho + 32, :].astype(BF16)


def _qkv_call(x2d, w_groups, pos4, inv_row, jobs):
    t = x2d.shape[0]
    nt = t // TILE
    x_specs = [pl.BlockSpec((TILE, LANES), functools.partial(lambda i, c: (i, c), c=c)) for c in range(8)]
    out_shapes, out_specs = [], []
    for d in DILATIONS:
        for _ in range(3):
            out_shapes.append(jax.ShapeDtypeStruct((t // d, d * GROUP_COL), BF16))
            out_specs.append(pl.BlockSpec((TILE // d, d * GROUP_COL), lambda i: (i, 0)))
    outs = pl.pallas_call(
        functools.partial(_qkv_kernel, n_casts=len(jobs), n_steps=nt, jobs=jobs),
        out_shape=tuple(out_shapes) + tuple(j.out_shape for j in jobs),
        grid=(nt,),
        in_specs=x_specs + [_resident(w.shape) for w in w_groups]
                 + [pl.BlockSpec((TILE // 4, LANES), lambda i: (i, 0)), _resident(inv_row.shape)]
                 + [j.in_spec for j in jobs],
        out_specs=tuple(out_specs) + tuple(j.out_spec for j in jobs),
        compiler_params=pltpu.CompilerParams(dimension_semantics=("arbitrary",), vmem_limit_bytes=VMEM_LIMIT),
        name="qkv_rope",
    )(*([x2d] * 8), *w_groups, pos4, inv_row, *[j.src for j in jobs])
    return outs[:9], outs[9:]


def _mask16(cond):
    return jnp.where(cond, 1.0, 0.0).astype(BF16) > 0


def _attn_block(q, keys_of, vals_of, bias, state_get, state_put, emit):
    lo16 = _mask16(lax.broadcasted_iota(jnp.int32, (ATTN_BLK, LANES), 1) < HEAD_DIM)
    lo = lax.broadcasted_iota(jnp.int32, (1, LANES), 1) < HEAD_DIM
    for j in range(HEADS // 2):
        sl = slice(LANES * j, LANES * (j + 1))
        qj = q[:, sl]
        zero = jnp.zeros_like(qj)
        lhs = jnp.concatenate([jnp.where(lo16, qj, zero), jnp.where(lo16, zero, qj)], axis=0)
        scores = lax.dot_general(lhs, keys_of(sl), (((1,), (1,)), ((), ())),
                                 preferred_element_type=F32)
        vals = vals_of(sl)
        rhs = jnp.concatenate([vals, jnp.ones_like(vals)], axis=1)
        outs, maxes = [], []
        for hl in range(2):
            s = scores[ATTN_BLK * hl:ATTN_BLK * (hl + 1), :] + bias
            mb = jnp.max(s, axis=1, keepdims=True)
            p = jnp.exp(s - mb).astype(BF16)
            outs.append(jnp.dot(p, rhs, preferred_element_type=F32))
            maxes.append(mb)
        o_new = jnp.where(lo, outs[0][:, :LANES], outs[1][:, :LANES])
        l_new = jnp.where(lo, outs[0][:, LANES:], outs[1][:, LANES:])
        m_new = jnp.where(lo, maxes[0], maxes[1])
        old = state_get(j)
        if old is not None:
            acc0, l0, m0 = old
            m = jnp.maximum(m0, m_new)
            a0 = jnp.exp(m0 - m)
            a1 = jnp.exp(m_new - m)
            o_new = acc0 * a0 + o_new * a1
            l_new = l0 * a0 + l_new * a1
            m_new = m
        if emit is not None:
            emit(j, o_new / l_new)
        else:
            state_put(j, o_new, l_new, m_new)


def _attn_kernel(q1, k1, v1, kp1, vp1, q2, k2, v2, kp2, vp2, q3, k3, v3, kp3, vp3, mask_ref, out_ref,
                 acc_s, l_s, m_s):
    n = pl.program_id(1)
    s = pl.program_id(2)
    state = (acc_s, l_s, m_s)

    def lanes(j):
        return slice(LANES * j, LANES * (j + 1))

    def bias_of(g, first):
        b = mask_ref[g]
        if first is None:
            return b
        prev_half = lax.broadcasted_iota(jnp.int32, (ATTN_BLK, 2 * ATTN_BLK), 1) < ATTN_BLK
        return jnp.where(jnp.logical_and(first, prev_half), NEG, b)

    def two_blocks(prev_ref, cur_ref, col0):
        def load(sl):
            cols = slice(col0 + sl.start, col0 + sl.stop)
            return jnp.concatenate([prev_ref[:, cols], cur_ref[:, cols]], axis=0)
        return load

    @pl.when(s < 4)
    def _group1():
        for c in range(4):
            k_blk = pl.multiple_of((4 * s + c) * SUBLANES, SUBLANES)

            def put(j, *vals, k_blk=k_blk):
                for ref, val in zip(state, vals):
                    for rho in range(N_RES):
                        ref[rho, pl.ds(k_blk, SUBLANES), lanes(j)] = val[SUBLANES * rho:SUBLANES * (rho + 1), :]

            rows = slice(ATTN_BLK * c, ATTN_BLK * (c + 1))
            if c == 0:
                keys_of = lambda sl: jnp.concatenate([kp1[:, sl], k1[:ATTN_BLK, sl]], axis=0)
                vals_of = lambda sl: jnp.concatenate([vp1[:, sl], v1[:ATTN_BLK, sl]], axis=0)
                first = jnp.logical_and(n == 0, s == 0)
            else:
                both = slice(ATTN_BLK * (c - 1), ATTN_BLK * (c + 1))
                keys_of = lambda sl, both=both: k1[both, sl]
                vals_of = lambda sl, both=both: v1[both, sl]
                first = None
            _attn_block(q1[rows, :], keys_of, vals_of, bias_of(0, first), lambda j: None, put, None)

    @pl.when(jnp.logical_and(s >= 4, s < 8))
    def _group2():
        t = s - 4
        base = pl.multiple_of(t * 32, 32)
        bias = bias_of(1, jnp.logical_and(n == 0, t == 0))
        for a in range(4):
            def get(j, a=a):
                return tuple(jnp.concatenate([ref[4 * b + a, pl.ds(base, 32), lanes(j)] for b in range(4)], axis=0)
                             for ref in state)

            def put(j, *vals, a=a):
                for ref, val in zip(state, vals):
                    for b in range(4):
                        ref[4 * b + a, pl.ds(base, 32), lanes(j)] = val[32 * b:32 * (b + 1), :]

            col0 = GROUP_COL * a
            _attn_block(q2[:, col0:col0 + GROUP_COL], two_blocks(kp2, k2, col0), two_blocks(vp2, v2, col0),
                        bias, get, put, None)

    @pl.when(s >= 8)
    def _group3():
        t = s - 8
        bias = bias_of(2, n == 0)
        for c in range(4):
            rho = 4 * t + c

            def get(j, rho=rho):
                return tuple(ref[rho, :, lanes(j)] for ref in state)

            def emit(j, y, c=c):
                out_ref[0, c, :, lanes(j)] = y.astype(BF16)

            col0 = GROUP_COL * c
            _attn_block(q3[:, col0:col0 + GROUP_COL], two_blocks(kp3, k3, col0), two_blocks(vp3, v3, col0),
                        bias, get, None, emit)


def _attn_call(qkv, masks, batch, seq):
    nchunk = seq // CHUNK
    tiles = seq // TILE

    def step(s, g):
        return jnp.clip(s - 4 * g, 0, 3)

    def cur1(b, n, s):
        return (b * tiles + 4 * n + step(s, 0), 0)

    def prev1(b, n, s):
        return (jnp.maximum(4 * (b * tiles + 4 * n + step(s, 0)) - 1, 0), 0)

    def cur2(b, n, s):
        return (b * tiles + 4 * n + step(s, 1), 0)

    def prev2(b, n, s):
        return (jnp.maximum(b * tiles + 4 * n + step(s, 1) - 1, 0), 0)

    def cur3(b, n, s):
        return (b * nchunk + n, step(s, 2))

    def prev3(b, n, s):
        return (jnp.maximum(b * nchunk + n - 1, 0), step(s, 2))

    big = (ATTN_BLK, 4 * GROUP_COL)
    in_specs = [
        pl.BlockSpec((TILE, GROUP_COL), cur1), pl.BlockSpec((TILE, GROUP_COL), cur1), pl.BlockSpec((TILE, GROUP_COL), cur1),
        pl.BlockSpec((ATTN_BLK, GROUP_COL), prev1), pl.BlockSpec((ATTN_BLK, GROUP_COL), prev1),
        pl.BlockSpec(big, cur2), pl.BlockSpec(big, cur2), pl.BlockSpec(big, cur2),
        pl.BlockSpec(big, prev2), pl.BlockSpec(big, prev2),
        pl.BlockSpec(big, cur3), pl.BlockSpec(big, cur3), pl.BlockSpec(big, cur3),
        pl.BlockSpec(big, prev3), pl.BlockSpec(big, prev3),
        _resident(masks.shape),
    ]
    q1, k1, v1, q2, k2, v2, q3, k3, v3 = qkv
    return pl.pallas_call(
        _attn_kernel,
        out_shape=jax.ShapeDtypeStruct((batch, N_RES, seq // N_RES, GROUP_COL), BF16),
        grid=(batch, nchunk, 12),
        in_specs=in_specs,
        out_specs=pl.BlockSpec((1, 4, ATTN_BLK, GROUP_COL), lambda b, n, s: (b, step(s, 2), n, 0)),
        scratch_shapes=[pltpu.VMEM((N_RES, ATTN_BLK, GROUP_COL), F32)] * 3,
        compiler_params=pltpu.CompilerParams(dimension_semantics=("arbitrary",) * 3, vmem_limit_bytes=VMEM_LIMIT),
        name="dilated_attention",
    )(q1, k1, v1, k1, v1, q2, k2, v2, k2, v2, q3, k3, v3, k3, v3, masks)


def _mix_kernel(*refs, tiles_per_batch, n_casts, n_steps, jobs):
    x_ref, y_ref, w_ref, vec_ref, ws_ref, bsb_ref, pa_ref, pb_ref, pc_ref, wo_ref = refs[:10]
    cast_src = refs[10:10 + n_casts]
    o_ref = refs[10 + n_casts]
    cast_dst = refs[11 + n_casts:11 + 2 * n_casts]
    zc_ref, ys_ref = refs[11 + 2 * n_casts:]
    i = pl.program_id(0)
    _run_casts(jobs, cast_src, cast_dst, i, n_steps)

    x = x_ref[...]
    xb = x.astype(BF16)

    def vec(r):
        return vec_ref[r:r + 1, :]

    def proj(k):
        return jnp.dot(xb, w_ref[:, D_MODEL * k:D_MODEL * (k + 1)], preferred_element_type=F32)

    @pl.when(i % tiles_per_batch == 0)
    def _():
        zc_ref[...] = jnp.zeros_like(zc_ref)

    z = proj(4) * proj(5)
    head = jnp.concatenate([zc_ref[...], z[:SUBLANES, :]], axis=0)
    zc_ref[...] = z[TILE - SUBLANES:, :]
    conv = vec(_P_CONV + 2) * z
    for shift in (1, 2):
        zs = jnp.concatenate([pltpu.roll(head, shift, 0)[SUBLANES:, :], pltpu.roll(z, shift, 0)[SUBLANES:, :]], axis=0)
        conv = conv + vec(_P_CONV + 2 - shift) * zs
    ya = (proj(3) * conv).astype(BF16)
    m = jax.nn.sigmoid(proj(0)) * jnp.dot(ya, pa_ref[...], preferred_element_type=F32)

    u = _gelu(proj(6))
    vb = _layer_norm(_gelu(proj(7)), vec(_P_GLN_G), vec(_P_GLN_B)).astype(BF16)
    n_chunks = TILE // GMLP_CHUNK
    tril = (lax.broadcasted_iota(jnp.int32, (GMLP_CHUNK, GMLP_CHUNK), 0)
            >= lax.broadcasted_iota(jnp.int32, (GMLP_CHUNK, GMLP_CHUNK), 1))
    cols = []
    for g in range(GMLP_GROUPS):
        gl = slice(LANES * g, LANES * (g + 1))
        wg = jnp.where(tril, ws_ref[g], 0.0).astype(BF16)
        rhs = jnp.concatenate([vb[GMLP_CHUNK * c:GMLP_CHUNK * (c + 1), gl] for c in range(n_chunks)], axis=1)
        sp = jnp.dot(wg, rhs, preferred_element_type=F32)
        bias = bsb_ref[:, gl]
        cols.append(jnp.concatenate([sp[:, LANES * c:LANES * (c + 1)] + bias for c in range(n_chunks)], axis=0))
    yc = (u * jnp.concatenate(cols, axis=1)).astype(BF16)
    m = m + jax.nn.sigmoid(proj(2)) * jnp.dot(yc, pc_ref[...], preferred_element_type=F32)

    for rho in range(N_RES):
        piece = y_ref[0, rho].astype(F32)
        for sl in range(GROUP_COL // LANES):
            ys_ref[sl, pl.ds(rho, ROWS_PER_RES, stride=N_RES), :] = piece[:, LANES * sl:LANES * (sl + 1)]
    yb = jnp.concatenate([ys_ref[sl] for sl in range(GROUP_COL // LANES)], axis=1).astype(BF16)
    m = m + jax.nn.sigmoid(proj(1)) * jnp.dot(yb, pb_ref[...], preferred_element_type=F32)

    mixed = jnp.dot(m.astype(BF16), wo_ref[...], preferred_element_type=F32)
    o_ref[...] = _layer_norm(ALPHA * x + mixed, vec(_P_LN1_G), vec(_P_LN1_B))


def _mix_call(x2d, y_attn, w_rest, vecs, w_s, bsb, pa, pb, pc, wo, layer, seq, jobs):
    t = x2d.shape[0]
    nt = t // TILE
    tiles_per_batch = seq // TILE
    outs = pl.pallas_call(
        functools.partial(_mix_kernel, tiles_per_batch=tiles_per_batch, n_casts=len(jobs), n_steps=nt, jobs=jobs),
        out_shape=(jax.ShapeDtypeStruct((t, D_MODEL), F32),) + tuple(j.out_shape for j in jobs),
        grid=(nt,),
        in_specs=[pl.BlockSpec((TILE, D_MODEL), lambda i: (i, 0)),
                  pl.BlockSpec((1, N_RES, ROWS_PER_RES, GROUP_COL),
                               lambda i: (i // tiles_per_batch, 0, i % tiles_per_batch, 0)),
                  _resident(w_rest.shape), _layer_resident(vecs.shape, layer), _layer_resident(w_s.shape, layer),
                  _layer_resident(bsb.shape, layer)]
                 + [_resident(w.shape) for w in (pa, pb, pc, wo)] + [j.in_spec for j in jobs],
        out_specs=(pl.BlockSpec((TILE, D_MODEL), lambda i: (i, 0)),) + tuple(j.out_spec for j in jobs),
        scratch_shapes=[pltpu.VMEM((SUBLANES, D_MODEL), F32),
                        pltpu.VMEM((GROUP_COL // LANES, TILE, LANES), F32)],
        compiler_params=pltpu.CompilerParams(dimension_semantics=("arbitrary",), vmem_limit_bytes=VMEM_LIMIT),
        name="mixers_out_ln",
    )(x2d, y_attn, w_rest, vecs, w_s, bsb, pa, pb, pc, wo, *[j.src for j in jobs])
    return outs[0], outs[1:]


def _ffn_kernel(*refs, n_casts, n_steps, jobs):
    x_ref, wg_ref, wu_ref, wd_ref, vec_ref = refs[:5]
    cast_src = refs[5:5 + n_casts]
    o_ref = refs[5 + n_casts]
    cast_dst = refs[6 + n_casts:]
    _run_casts(jobs, cast_src, cast_dst, pl.program_id(0), n_steps)

    x = x_ref[...]
    xb = x.astype(BF16)
    gate = jnp.dot(xb, wg_ref[...], preferred_element_type=F32)
    up = jnp.dot(xb, wu_ref[...], preferred_element_type=F32)
    h = (jax.nn.silu(gate) * up).astype(BF16)
    down = jnp.dot(h, wd_ref[...], preferred_element_type=F32)
    o_ref[...] = _layer_norm(ALPHA * x + down, vec_ref[_P_LN2_G:_P_LN2_G + 1, :], vec_ref[_P_LN2_B:_P_LN2_B + 1, :])


def _ffn_call(x2d, wg, wu, wd, vecs, layer, jobs):
    t = x2d.shape[0]
    nt = t // TILE
    outs = pl.pallas_call(
        functools.partial(_ffn_kernel, n_casts=len(jobs), n_steps=nt, jobs=jobs),
        out_shape=(jax.ShapeDtypeStruct((t, D_MODEL), F32),) + tuple(j.out_shape for j in jobs),
        grid=(nt,),
        in_specs=[pl.BlockSpec((TILE, D_MODEL), lambda i: (i, 0))] + [_resident(w.shape) for w in (wg, wu, wd)]
                 + [_layer_resident(vecs.shape, layer)] + [j.in_spec for j in jobs],
        out_specs=(pl.BlockSpec((TILE, D_MODEL), lambda i: (i, 0)),) + tuple(j.out_spec for j in jobs),
        compiler_params=pltpu.CompilerParams(dimension_semantics=("arbitrary",), vmem_limit_bytes=VMEM_LIMIT),
        name="swiglu_ln",
    )(x2d, wg, wu, wd, vecs, *[j.src for j in jobs])
    return outs[0], outs[1:]


def _band_masks():
    u = np.arange(ATTN_BLK)
    stored = (
        (u % N_RES) * SUBLANES + u // N_RES,
        (u % 4) * 32 + u // 4,
        u,
    )
    out = np.empty((N_GROUPS, ATTN_BLK, 2 * ATTN_BLK), np.float32)
    for g, st in enumerate(stored):
        orig = np.empty(ATTN_BLK, np.int64)
        orig[st] = u
        qi = orig[:, None]
        ki = orig[None, :]
        out[g, :, :ATTN_BLK] = np.where(ki >= qi, 0.0, NEG)
        out[g, :, ATTN_BLK:] = np.where(ki <= qi, 0.0, NEG)
    return out


def kernel(x, positions, w_in, conv_w, gmlp_ln_g, gmlp_ln_b, w_s, b_s, p_a, p_b, p_c, w_o, ln1_g, ln1_b, w_gate, w_up,
           w_down, ln2_g, ln2_b):
    batch, seq, d = x.shape
    t = batch * seq
    nt = t // TILE
    depth = w_in.shape[0]
    assert d == D_MODEL and seq % CHUNK == 0 and depth == DEPTH and nt % 16 == 0
    n_cast = 32 if nt >= 32 else 16

    pos = positions.reshape(nt, ROWS_PER_RES, N_RES).transpose(0, 2, 1).reshape(nt, 4, TILE // 4).astype(F32)
    pos4 = jnp.repeat(pos.transpose(0, 2, 1), HALF, axis=2).reshape(t // 4, LANES)
    inv_freq = ROPE_THETA ** (-jnp.arange(HALF, dtype=F32) / HALF)
    inv_row = jnp.tile(inv_freq, LANES // HALF).reshape(1, LANES)

    masks = jnp.asarray(_band_masks())
    vecs = jnp.concatenate([conv_w] + [v[:, None, :] for v in (gmlp_ln_g, gmlp_ln_b, ln1_g, ln1_b, ln2_g, ln2_b)],
                           axis=1)
    bsb = jnp.repeat(jnp.swapaxes(b_s, 1, 2), GMLP_CHUNK, axis=2)

    rest_bw = _REST_COLS // n_cast
    gap = (_OFF_GMLP - _OFF_ATTN) // rest_bw

    def rest_col_block(i):
        return jnp.where(i < _OFF_ATTN // rest_bw, i, i + gap)

    def qkv_jobs(layer):
        return [_CastJob(w_in, layer, D_MODEL, _QKV_COLS, n_cast, col_block=_OFF_ATTN // _QKV_COLS + g)
                for g in range(N_GROUPS)]

    w_groups = _cast_call(qkv_jobs(0), n_cast)
    h = x.reshape(t, D_MODEL)
    for l in range(depth):
        jobs = [_CastJob(w_in, l, D_MODEL, _REST_COLS, n_cast, by_cols=True, col_map=rest_col_block),
                _CastJob(p_a, l, D_MODEL, D_MODEL, n_cast), _CastJob(p_b, l, GROUP_COL, D_MODEL, n_cast),
                _CastJob(p_c, l, D_MODEL, D_MODEL, n_cast), _CastJob(w_o, l, D_MODEL, D_MODEL, n_cast)]
        qkv, (w_rest, pa, pb, pc, wo) = _qkv_call(h, w_groups, pos4, inv_row, jobs)
        y_attn = _attn_call(qkv, masks, batch, seq)
        jobs = [_CastJob(w_gate, l, D_MODEL, D_FF, n_cast), _CastJob(w_up, l, D_MODEL, D_FF, n_cast),
                _CastJob(w_down, l, D_FF, D_MODEL, _FFN_DOWN_BLOCKS)]
        h, (wg, wu, wd) = _mix_call(h, y_attn, w_rest, vecs, w_s, bsb, pa, pb, pc, wo, l, seq, jobs)
        jobs = qkv_jobs(l + 1) if l + 1 < depth else []
        h, w_next = _ffn_call(h, wg, wu, wd, vecs, l, jobs)
        w_groups = tuple(w_next)
    return h.reshape(batch, seq, D_MODEL)
```

```python
import functools

import numpy as np
import jax
import jax.numpy as jnp
from jax import lax
from jax.experimental import pallas as pl
from jax.experimental.pallas import tpu as pltpu

F32 = jnp.float32
BF16 = jnp.bfloat16

D_MODEL = 1024
HEADS = 8
HEAD_DIM = 64
HALF = HEAD_DIM // 2
GROUP_COL = HEADS * HEAD_DIM
DILATIONS = (1, 4, 16)
N_GROUPS = len(DILATIONS)
ATTN_BLK = 128
N_RES = 16
CHUNK = N_RES * ATTN_BLK
STEP_BLOCKS = 8
_STEPS_PER_GROUP = N_RES // STEP_BLOCKS
GMLP_GROUPS = 8
GMLP_CHUNK = 128
D_FF = 2816
DEPTH = 2
ALPHA = (2 * DEPTH) ** 0.25
LN_EPS = 1e-5
ROPE_THETA = 10000.0
LOG2_E = 1.4426950408889634

TILE = 512
ROWS_PER_RES = TILE // N_RES
LANES = 128
SUBLANES = 8
NEG = -1e30
VMEM_LIMIT = 56 * 1024 * 1024

_OFF_ATTN = 6 * D_MODEL
_QKV_COLS = 3 * GROUP_COL
_OFF_GMLP = _OFF_ATTN + N_GROUPS * _QKV_COLS
_REST_COLS = _OFF_ATTN + 2 * D_MODEL
_FFN_DOWN_BLOCKS = 16

_P_CONV, _P_GLN_G, _P_GLN_B, _P_LN1_G, _P_LN1_B, _P_LN2_G, _P_LN2_B = 0, 3, 4, 5, 6, 7, 8
_P_ROWS = 9


def _resident(shape):
    nd = len(shape)
    return pl.BlockSpec(shape, lambda *_: (0,) * nd, pipeline_mode=pl.Buffered(1))


def _layer_resident(shape, layer):
    nd = len(shape)
    return pl.BlockSpec((None,) + tuple(shape[1:]), lambda *_: (layer,) + (0,) * (nd - 1),
                        pipeline_mode=pl.Buffered(1))


def _layer_norm(v, g, b):
    mu = jnp.mean(v, axis=-1, keepdims=True)
    c = v - mu
    var = jnp.mean(c * c, axis=-1, keepdims=True)
    return c * lax.rsqrt(var + LN_EPS) * g + b


def _gelu(v):
    return 0.5 * v * (1.0 + lax.erf(v * 0.7071067811865476))


class _CastJob:
    def __init__(self, src, layer, rows, cols, n_blocks, *, col_block=0, by_cols=False, col_map=None):
        self.src, self.n_blocks = src, n_blocks
        last = n_blocks - 1
        if by_cols:
            bw = cols // n_blocks
            self.in_spec = pl.BlockSpec((None, rows, bw), lambda i: (layer, 0, col_map(jnp.minimum(i, last))))
            self.out_spec = pl.BlockSpec((rows, bw), lambda i: (0, jnp.minimum(i, last)))
        else:
            rb = rows // n_blocks
            self.in_spec = pl.BlockSpec((None, rb, cols), lambda i: (layer, jnp.minimum(i, last), col_block))
            self.out_spec = pl.BlockSpec((rb, cols), lambda i: (jnp.minimum(i, last), 0))
        self.out_shape = jax.ShapeDtypeStruct((rows, cols), BF16)


def _run_casts(jobs, src_refs, dst_refs, step, n_steps):
    for job, s, d in zip(jobs, src_refs, dst_refs):
        if job.n_blocks >= n_steps:
            d[...] = s[...].astype(BF16)
        else:
            @pl.when(step < job.n_blocks)
            def _(s=s, d=d):
                d[...] = s[...].astype(BF16)


def _cast_kernel(*refs):
    n = len(refs) // 2
    for s, d in zip(refs[:n], refs[n:]):
        d[...] = s[...].astype(BF16)


def _cast_call(jobs, n_steps):
    return pl.pallas_call(
        _cast_kernel,
        out_shape=tuple(j.out_shape for j in jobs),
        grid=(n_steps,),
        in_specs=[j.in_spec for j in jobs],
        out_specs=tuple(j.out_spec for j in jobs),
        compiler_params=pltpu.CompilerParams(dimension_semantics=("arbitrary",)),
        name="cast_weights",
    )(*[j.src for j in jobs])


def _rope_tables(pos4, inv_row):
    ang = pos4 * inv_row
    lane = lax.broadcasted_iota(jnp.int32, (TILE // 4, LANES), 1)
    group = lane // HALF
    sign = jnp.where((lane % HEAD_DIM) < HALF, -1.0, 1.0)
    out = []
    for dense, sgn in ((jnp.cos(ang), None), (jnp.sin(ang), sign)):
        rolled = [dense] + [pltpu.roll(dense, HALF * k, 1) for k in range(1, 4)]
        slabs = []
        for a in range(4):
            e = rolled[(0 - a) % 4]
            for g in range(1, 4):
                e = jnp.where(group == g, rolled[(g - a) % 4], e)
            slabs.append(e if sgn is None else e * sgn)
        out.append(jnp.concatenate(slabs, axis=0))
    return out


def _qkv_kernel(*refs, n_casts, n_steps, jobs):
    x_refs = refs[:8]
    w_refs = refs[8:11]
    pos_ref, inv_ref = refs[11:13]
    cast_src = refs[13:13 + n_casts]
    out_refs = refs[13 + n_casts:22 + n_casts]
    cast_dst = refs[22 + n_casts:]
    _run_casts(jobs, cast_src, cast_dst, pl.program_id(0), n_steps)

    rows = [jnp.concatenate([xr[pl.ds(rho, ROWS_PER_RES, stride=N_RES), :] for xr in x_refs], axis=1)
            for rho in range(N_RES)]
    xp = jnp.concatenate(rows, axis=0).astype(BF16)
    cos, sin = _rope_tables(pos_ref[...], inv_ref[...])
    scale = HEAD_DIM ** -0.5 * LOG2_E
    tables = ((cos * scale, sin * scale), (cos, sin))
    first_half = (lax.broadcasted_iota(jnp.int32, (TILE, LANES), 1) % HEAD_DIM) < HALF

    for g in range(N_GROUPS):
        for j in range(3):
            y = jnp.dot(xp, w_refs[g][:, GROUP_COL * j:GROUP_COL * (j + 1)], preferred_element_type=F32)
            if j < 2:
                c, s = tables[j]
                parts = []
                for sl in range(GROUP_COL // LANES):
                    a = y[:, LANES * sl:LANES * (sl + 1)]
                    partner = jnp.where(first_half, pltpu.roll(a, LANES - HALF, 1), pltpu.roll(a, HALF, 1))
                    parts.append(a * c + partner * s)
                y = jnp.concatenate(parts, axis=1)
            o_ref = out_refs[3 * g + j]
            if g == 0:
                pieces = [y[32 * rho + 8 * c:32 * rho + 8 * c + 8, :] for c in range(4) for rho in range(N_RES)]
                o_ref[...] = jnp.concatenate(pieces, axis=0).astype(BF16)
            elif g == 1:
                for a in range(4):
                    slab = jnp.concatenate([y[32 * (4 * b + a):32 * (4 * b + a) + 32, :] for b in range(4)], axis=0)
                    o_ref[:, GROUP_COL * a:GROUP_COL * (a + 1)] = slab.astype(BF16)
            else:
                for rho in range(N_RES):
                    o_ref[:, GROUP_COL * rho:GROUP_COL * (rho + 1)] = y[32 * rho:32 * rho + 32, :].astype(BF16)


def _qkv_call(x2d, w_groups, pos4, inv_row, jobs):
    t = x2d.shape[0]
    nt = t // TILE
    x_specs = [pl.BlockSpec((TILE, LANES), functools.partial(lambda i, c: (i, c), c=c)) for c in range(8)]
    out_shapes, out_specs = [], []
    for d in DILATIONS:
        for _ in range(3):
            out_shapes.append(jax.ShapeDtypeStruct((t // d, d * GROUP_COL), BF16))
            out_specs.append(pl.BlockSpec((TILE // d, d * GROUP_COL), lambda i: (i, 0)))
    outs = pl.pallas_call(
        functools.partial(_qkv_kernel, n_casts=len(jobs), n_steps=nt, jobs=jobs),
        out_shape=tuple(out_shapes) + tuple(j.out_shape for j in jobs),
        grid=(nt,),
        in_specs=x_specs + [_resident(w.shape) for w in w_groups]
                 + [pl.BlockSpec((TILE // 4, LANES), lambda i: (i, 0)), _resident(inv_row.shape)]
                 + [j.in_spec for j in jobs],
        out_specs=tuple(out_specs) + tuple(j.out_spec for j in jobs),
        compiler_params=pltpu.CompilerParams(dimension_semantics=("arbitrary",), vmem_limit_bytes=VMEM_LIMIT),
        name="qkv_rope",
    )(*([x2d] * 8), *w_groups, pos4, inv_row, *[j.src for j in jobs])
    return outs[:9], outs[9:]


def _mask16(cond):
    return jnp.where(cond, 1.0, 0.0).astype(BF16) > 0


def _attn_block(q, keys_of, vals_of, bias_t, state_get, state_put, emit):
    lo16 = _mask16(lax.broadcasted_iota(jnp.int32, (ATTN_BLK, LANES), 1) < HEAD_DIM)
    lo = lax.broadcasted_iota(jnp.int32, (1, LANES), 1) < HEAD_DIM
    eye = jnp.where(lax.broadcasted_iota(jnp.int32, (2 * ATTN_BLK, LANES), 0) % ATTN_BLK
                    == lax.broadcasted_iota(jnp.int32, (2 * ATTN_BLK, LANES), 1), 1.0, 0.0).astype(BF16)
    for j in range(HEADS // 2):
        sl = slice(LANES * j, LANES * (j + 1))
        qj = q[:, sl]
        zero = jnp.zeros_like(qj)
        lhs = jnp.concatenate([jnp.where(lo16, qj, zero), jnp.where(lo16, zero, qj)], axis=0)
        scores = lax.dot_general(jnp.concatenate([lhs, eye], axis=1),
                                 jnp.concatenate([keys_of(sl), bias_t], axis=1),
                                 (((1,), (1,)), ((), ())), preferred_element_type=F32)
        vals = vals_of(sl)
        rhs = jnp.concatenate([vals, jnp.ones_like(vals)], axis=1)
        outs, maxes = [], []
        for hl in range(2):
            s = scores[ATTN_BLK * hl:ATTN_BLK * (hl + 1), :]
            mb = jnp.max(s, axis=1, keepdims=True)
            p = jnp.exp2(s - mb).astype(BF16)
            outs.append(jnp.dot(p, rhs, preferred_element_type=F32))
            maxes.append(mb)
        o_new = jnp.where(lo, outs[0][:, :LANES], outs[1][:, :LANES])
        l_new = jnp.where(lo, outs[0][:, LANES:], outs[1][:, LANES:])
        m_new = jnp.where(lo, maxes[0], maxes[1])
        old = state_get(j)
        if old is not None:
            acc0, l0, m0 = old
            m = jnp.maximum(m0, m_new)
            a0 = jnp.exp2(m0 - m)
            a1 = jnp.exp2(m_new - m)
            o_new = acc0 * a0 + o_new * a1
            l_new = l0 * a0 + l_new * a1
            m_new = m
        if emit is not None:
            emit(j, o_new / l_new)
        else:
            state_put(j, o_new, l_new, m_new)


def _attn_kernel(q1, k1, v1, kp1, vp1, q2, k2, v2, kp2, vp2, q3, k3, v3, kp3, vp3, mask_ref, out_ref,
                 acc_s, l_s, m_s):
    n = pl.program_id(1)
    s = pl.program_id(2)
    state = (acc_s, l_s, m_s)

    def lanes(j):
        return slice(LANES * j, LANES * (j + 1))

    def bias_of(g, first):
        return mask_ref[g, 0] if first is None else mask_ref[g, first.astype(jnp.int32)]

    def two_blocks(prev_ref, cur_ref, col0):
        def load(sl):
            cols = slice(col0 + sl.start, col0 + sl.stop)
            return jnp.concatenate([prev_ref[:, cols], cur_ref[:ATTN_BLK, cols]], axis=0)
        return load

    spg = _STEPS_PER_GROUP

    @pl.when(s < spg)
    def _group1():
        for c in range(STEP_BLOCKS):
            k_blk = pl.multiple_of((STEP_BLOCKS * s + c) * SUBLANES, SUBLANES)

            def put(j, *vals, k_blk=k_blk):
                for ref, val in zip(state, vals):
                    for rho in range(N_RES):
                        ref[rho, pl.ds(k_blk, SUBLANES), lanes(j)] = val[SUBLANES * rho:SUBLANES * (rho + 1), :]

            rows = slice(ATTN_BLK * c, ATTN_BLK * (c + 1))
            if c == 0:
                keys_of = lambda sl: jnp.concatenate([kp1[:, sl], k1[:ATTN_BLK, sl]], axis=0)
                vals_of = lambda sl: jnp.concatenate([vp1[:, sl], v1[:ATTN_BLK, sl]], axis=0)
                first = jnp.logical_and(n == 0, s == 0)
            else:
                both = slice(ATTN_BLK * (c - 1), ATTN_BLK * (c + 1))
                keys_of = lambda sl, both=both: k1[both, sl]
                vals_of = lambda sl, both=both: v1[both, sl]
                first = None
            _attn_block(q1[rows, :], keys_of, vals_of, bias_of(0, first), lambda j: None, put, None)

    @pl.when(jnp.logical_and(s >= spg, s < 2 * spg))
    def _group2():
        t = s - spg
        for kk in range(STEP_BLOCKS // 4):
            base = pl.multiple_of((t * (STEP_BLOCKS // 4) + kk) * 32, 32)
            bias = bias_of(1, jnp.logical_and(n == 0, t == 0)) if kk == 0 else bias_of(1, None)
            rows = slice(ATTN_BLK * kk, ATTN_BLK * (kk + 1))
            both = slice(ATTN_BLK * (kk - 1), ATTN_BLK * (kk + 1))
            for a in range(4):
                def get(j, a=a, base=base):
                    return tuple(jnp.concatenate([ref[4 * b + a, pl.ds(base, 32), lanes(j)] for b in range(4)],
                                                 axis=0) for ref in state)

                def put(j, *vals, a=a, base=base):
                    for ref, val in zip(state, vals):
                        for b in range(4):
                            ref[4 * b + a, pl.ds(base, 32), lanes(j)] = val[32 * b:32 * (b + 1), :]

                col0 = GROUP_COL * a
                if kk == 0:
                    keys_of = two_blocks(kp2, k2, col0)
                    vals_of = two_blocks(vp2, v2, col0)
                else:
                    keys_of = lambda sl, col0=col0, both=both: k2[both, col0 + sl.start:col0 + sl.stop]
                    vals_of = lambda sl, col0=col0, both=both: v2[both, col0 + sl.start:col0 + sl.stop]
                _attn_block(q2[rows, col0:col0 + GROUP_COL], keys_of, vals_of, bias, get, put, None)

    @pl.when(s >= 2 * spg)
    def _group3():
        t = s - 2 * spg
        bias = bias_of(2, n == 0)
        for c in range(STEP_BLOCKS):
            rho = STEP_BLOCKS * t + c

            def get(j, rho=rho):
                return tuple(ref[rho, :, lanes(j)] for ref in state)

            def emit(j, y, c=c):
                out_ref[0, c, :, lanes(j)] = y.astype(BF16)

            col0 = GROUP_COL * c
            _attn_block(q3[:, col0:col0 + GROUP_COL], two_blocks(kp3, k3, col0), two_blocks(vp3, v3, col0),
                        bias, get, None, emit)


def _attn_call(qkv, masks, batch, seq):
    nchunk = seq // CHUNK
    spg = _STEPS_PER_GROUP
    rows1 = STEP_BLOCKS * ATTN_BLK
    rows2 = (STEP_BLOCKS // 4) * ATTN_BLK

    def step(s, g):
        return jnp.clip(s - spg * g, 0, spg - 1)

    def cur1(b, n, s):
        return (b * (seq // rows1) + spg * n + step(s, 0), 0)

    def prev1(b, n, s):
        return (jnp.maximum(STEP_BLOCKS * cur1(b, n, s)[0] - 1, 0), 0)

    def cur2(b, n, s):
        return (b * (seq // 4 // rows2) + spg * n + step(s, 1), 0)

    def prev2(b, n, s):
        return (jnp.maximum((STEP_BLOCKS // 4) * cur2(b, n, s)[0] - 1, 0), 0)

    def cur3(b, n, s):
        return (b * nchunk + n, step(s, 2))

    def prev3(b, n, s):
        return (jnp.maximum(b * nchunk + n - 1, 0), step(s, 2))

    blk1 = (rows1, GROUP_COL)
    blk2 = (rows2, 4 * GROUP_COL)
    prev2_blk = (ATTN_BLK, 4 * GROUP_COL)
    blk3 = (ATTN_BLK, STEP_BLOCKS * GROUP_COL)
    in_specs = [
        pl.BlockSpec(blk1, cur1), pl.BlockSpec(blk1, cur1), pl.BlockSpec(blk1, cur1),
        pl.BlockSpec((ATTN_BLK, GROUP_COL), prev1), pl.BlockSpec((ATTN_BLK, GROUP_COL), prev1),
        pl.BlockSpec(blk2, cur2), pl.BlockSpec(blk2, cur2), pl.BlockSpec(blk2, cur2),
        pl.BlockSpec(prev2_blk, prev2), pl.BlockSpec(prev2_blk, prev2),
        pl.BlockSpec(blk3, cur3), pl.BlockSpec(blk3, cur3), pl.BlockSpec(blk3, cur3),
        pl.BlockSpec(blk3, prev3), pl.BlockSpec(blk3, prev3),
        _resident(masks.shape),
    ]
    q1, k1, v1, q2, k2, v2, q3, k3, v3 = qkv
    return pl.pallas_call(
        _attn_kernel,
        out_shape=jax.ShapeDtypeStruct((batch, N_RES, seq // N_RES, GROUP_COL), BF16),
        grid=(batch, nchunk, N_GROUPS * spg),
        in_specs=in_specs,
        out_specs=pl.BlockSpec((1, STEP_BLOCKS, ATTN_BLK, GROUP_COL), lambda b, n, s: (b, step(s, 2), n, 0)),
        scratch_shapes=[pltpu.VMEM((N_RES, ATTN_BLK, GROUP_COL), F32)] * 3,
        compiler_params=pltpu.CompilerParams(dimension_semantics=("arbitrary",) * 3, vmem_limit_bytes=VMEM_LIMIT),
        name="dilated_attention",
    )(q1, k1, v1, k1, v1, q2, k2, v2, k2, v2, q3, k3, v3, k3, v3, masks)


def _mix_kernel(*refs, tiles_per_batch, n_casts, n_steps, jobs):
    x_ref, y_ref, w_ref, vec_ref, ws_ref, bsb_ref, pa_ref, pb_ref, pc_ref, wo_ref = refs[:10]
    cast_src = refs[10:10 + n_casts]
    o_ref = refs[10 + n_casts]
    cast_dst = refs[11 + n_casts:11 + 2 * n_casts]
    zc_ref, ys_ref = refs[11 + 2 * n_casts:]
    i = pl.program_id(0)
    _run_casts(jobs, cast_src, cast_dst, i, n_steps)

    x = x_ref[...]
    xb = x.astype(BF16)

    def vec(r):
        return vec_ref[r:r + 1, :]

    def proj(k):
        return jnp.dot(xb, w_ref[:, D_MODEL * k:D_MODEL * (k + 1)], preferred_element_type=F32)

    @pl.when(i % tiles_per_batch == 0)
    def _():
        zc_ref[...] = jnp.zeros_like(zc_ref)

    z = proj(4) * proj(5)
    head = jnp.concatenate([zc_ref[...], z[:SUBLANES, :]], axis=0)
    zc_ref[...] = z[TILE - SUBLANES:, :]
    conv = vec(_P_CONV + 2) * z
    for shift in (1, 2):
        zs = jnp.concatenate([pltpu.roll(head, shift, 0)[SUBLANES:, :], pltpu.roll(z, shift, 0)[SUBLANES:, :]], axis=0)
        conv = conv + vec(_P_CONV + 2 - shift) * zs
    ya = (proj(3) * conv).astype(BF16)
    m = jax.nn.sigmoid(proj(0)) * jnp.dot(ya, pa_ref[...], preferred_element_type=F32)

    u = _gelu(proj(6))
    vb = _layer_norm(_gelu(proj(7)), vec(_P_GLN_G), vec(_P_GLN_B)).astype(BF16)
    n_chunks = TILE // GMLP_CHUNK
    tril = (lax.broadcasted_iota(jnp.int32, (GMLP_CHUNK, GMLP_CHUNK), 0)
            >= lax.broadcasted_iota(jnp.int32, (GMLP_CHUNK, GMLP_CHUNK), 1))
    cols = []
    for g in range(GMLP_GROUPS):
        gl = slice(LANES * g, LANES * (g + 1))
        wg = jnp.where(tril, ws_ref[g], 0.0).astype(BF16)
        rhs = jnp.concatenate([vb[GMLP_CHUNK * c:GMLP_CHUNK * (c + 1), gl] for c in range(n_chunks)], axis=1)
        sp = jnp.dot(wg, rhs, preferred_element_type=F32)
        bias = bsb_ref[:, gl]
        cols.append(jnp.concatenate([sp[:, LANES * c:LANES * (c + 1)] + bias for c in range(n_chunks)], axis=0))
    yc = (u * jnp.concatenate(cols, axis=1)).astype(BF16)
    m = m + jax.nn.sigmoid(proj(2)) * jnp.dot(yc, pc_ref[...], preferred_element_type=F32)

    for rho in range(N_RES):
        piece = y_ref[0, rho].astype(F32)
        for sl in range(GROUP_COL // LANES):
            ys_ref[sl, pl.ds(rho, ROWS_PER_RES, stride=N_RES), :] = piece[:, LANES * sl:LANES * (sl + 1)]
    yb = jnp.concatenate([ys_ref[sl] for sl in range(GROUP_COL // LANES)], axis=1).astype(BF16)
    m = m + jax.nn.sigmoid(proj(1)) * jnp.dot(yb, pb_ref[...], preferred_element_type=F32)

    mixed = jnp.dot(m.astype(BF16), wo_ref[...], preferred_element_type=F32)
    o_ref[...] = _layer_norm(ALPHA * x + mixed, vec(_P_LN1_G), vec(_P_LN1_B))


def _mix_call(x2d, y_attn, w_rest, vecs, w_s, bsb, pa, pb, pc, wo, layer, seq, jobs):
    t = x2d.shape[0]
    nt = t // TILE
    tiles_per_batch = seq // TILE
    outs = pl.pallas_call(
        functools.partial(_mix_kernel, tiles_per_batch=tiles_per_batch, n_casts=len(jobs), n_steps=nt, jobs=jobs),
        out_shape=(jax.ShapeDtypeStruct((t, D_MODEL), F32),) + tuple(j.out_shape for j in jobs),
        grid=(nt,),
        in_specs=[pl.BlockSpec((TILE, D_MODEL), lambda i: (i, 0)),
                  pl.BlockSpec((1, N_RES, ROWS_PER_RES, GROUP_COL),
                               lambda i: (i // tiles_per_batch, 0, i % tiles_per_batch, 0)),
                  _resident(w_rest.shape), _layer_resident(vecs.shape, layer), _layer_resident(w_s.shape, layer),
                  _layer_resident(bsb.shape, layer)]
                 + [_resident(w.shape) for w in (pa, pb, pc, wo)] + [j.in_spec for j in jobs],
        out_specs=(pl.BlockSpec((TILE, D_MODEL), lambda i: (i, 0)),) + tuple(j.out_spec for j in jobs),
        scratch_shapes=[pltpu.VMEM((SUBLANES, D_MODEL), F32),
                        pltpu.VMEM((GROUP_COL // LANES, TILE, LANES), F32)],
        compiler_params=pltpu.CompilerParams(dimension_semantics=("arbitrary",), vmem_limit_bytes=VMEM_LIMIT),
        name="mixers_out_ln",
    )(x2d, y_attn, w_rest, vecs, w_s, bsb, pa, pb, pc, wo, *[j.src for j in jobs])
    return outs[0], outs[1:]


def _ffn_kernel(*refs, n_casts, n_steps, jobs):
    x_ref, wg_ref, wu_ref, wd_ref, vec_ref = refs[:5]
    cast_src = refs[5:5 + n_casts]
    o_ref = refs[5 + n_casts]
    cast_dst = refs[6 + n_casts:]
    _run_casts(jobs, cast_src, cast_dst, pl.program_id(0), n_steps)

    x = x_ref[...]
    xb = x.astype(BF16)
    gate = jnp.dot(xb, wg_ref[...], preferred_element_type=F32)
    up = jnp.dot(xb, wu_ref[...], preferred_element_type=F32)
    h = (jax.nn.silu(gate) * up).astype(BF16)
    down = jnp.dot(h, wd_ref[...], preferred_element_type=F32)
    o_ref[...] = _layer_norm(ALPHA * x + down, vec_ref[_P_LN2_G:_P_LN2_G + 1, :], vec_ref[_P_LN2_B:_P_LN2_B + 1, :])


def _ffn_call(x2d, wg, wu, wd, vecs, layer, jobs):
    t = x2d.shape[0]
    nt = t // TILE
    outs = pl.pallas_call(
        functools.partial(_ffn_kernel, n_casts=len(jobs), n_steps=nt, jobs=jobs),
        out_shape=(jax.ShapeDtypeStruct((t, D_MODEL), F32),) + tuple(j.out_shape for j in jobs),
        grid=(nt,),
        in_specs=[pl.BlockSpec((TILE, D_MODEL), lambda i: (i, 0))] + [_resident(w.shape) for w in (wg, wu, wd)]
                 + [_layer_resident(vecs.shape, layer)] + [j.in_spec for j in jobs],
        out_specs=(pl.BlockSpec((TILE, D_MODEL), lambda i: (i, 0)),) + tuple(j.out_spec for j in jobs),
        compiler_params=pltpu.CompilerParams(dimension_semantics=("arbitrary",), vmem_limit_bytes=VMEM_LIMIT),
        name="swiglu_ln",
    )(x2d, wg, wu, wd, vecs, *[j.src for j in jobs])
    return outs[0], outs[1:]


def _band_masks():
    u = np.arange(ATTN_BLK)
    stored = (
        (u % N_RES) * SUBLANES + u // N_RES,
        (u % 4) * 32 + u // 4,
        u,
    )
    out = np.empty((N_GROUPS, 2, 2 * ATTN_BLK, ATTN_BLK), np.float32)
    for g, st in enumerate(stored):
        orig = np.empty(ATTN_BLK, np.int64)
        orig[st] = u
        ki = orig[:, None]
        qi = orig[None, :]
        out[g, 0, :ATTN_BLK] = np.where(ki >= qi, 0.0, NEG)
        out[g, 1, :ATTN_BLK] = NEG
        out[g, :, ATTN_BLK:] = np.where(ki <= qi, 0.0, NEG)
    return out


def kernel(x, positions, w_in, conv_w, gmlp_ln_g, gmlp_ln_b, w_s, b_s, p_a, p_b, p_c, w_o, ln1_g, ln1_b, w_gate, w_up,
           w_down, ln2_g, ln2_b):
    batch, seq, d = x.shape
    t = batch * seq
    nt = t // TILE
    depth = w_in.shape[0]
    assert d == D_MODEL and seq % CHUNK == 0 and depth == DEPTH and nt % 16 == 0
    n_cast = 32 if nt >= 32 else 16

    pos = positions.reshape(nt, ROWS_PER_RES, N_RES).transpose(0, 2, 1).reshape(nt, 4, TILE // 4).astype(F32)
    pos4 = jnp.repeat(pos.transpose(0, 2, 1), HALF, axis=2).reshape(t // 4, LANES)
    inv_freq = ROPE_THETA ** (-jnp.arange(HALF, dtype=F32) / HALF)
    inv_row = jnp.tile(inv_freq, LANES // HALF).reshape(1, LANES)

    masks = jnp.asarray(_band_masks(), dtype=BF16)
    vecs = jnp.concatenate([conv_w] + [v[:, None, :] for v in (gmlp_ln_g, gmlp_ln_b, ln1_g, ln1_b, ln2_g, ln2_b)],
                           axis=1)
    bsb = jnp.repeat(jnp.swapaxes(b_s, 1, 2), GMLP_CHUNK, axis=2)

    rest_bw = _REST_COLS // n_cast
    gap = (_OFF_GMLP - _OFF_ATTN) // rest_bw

    def rest_col_block(i):
        return jnp.where(i < _OFF_ATTN // rest_bw, i, i + gap)

    def qkv_jobs(layer, n_blocks):
        return [_CastJob(w_in, layer, D_MODEL, _QKV_COLS, n_blocks, col_block=_OFF_ATTN // _QKV_COLS + g)
                for g in range(N_GROUPS)]

    w_groups = _cast_call(qkv_jobs(0, SUBLANES), SUBLANES)
    h = x.reshape(t, D_MODEL)
    for l in range(depth):
        jobs = [_CastJob(w_in, l, D_MODEL, _REST_COLS, n_cast, by_cols=True, col_map=rest_col_block),
                _CastJob(p_a, l, D_MODEL, D_MODEL, n_cast), _CastJob(p_b, l, GROUP_COL, D_MODEL, n_cast),
                _CastJob(p_c, l, D_MODEL, D_MODEL, n_cast), _CastJob(w_o, l, D_MODEL, D_MODEL, n_cast)]
        qkv, (w_rest, pa, pb, pc, wo) = _qkv_call(h, w_groups, pos4, inv_row, jobs)
        y_attn = _attn_call(qkv, masks, batch, seq)
        jobs = [_CastJob(w_gate, l, D_MODEL, D_FF, n_cast), _CastJob(w_up, l, D_MODEL, D_FF, n_cast),
                _CastJob(w_down, l, D_FF, D_MODEL, _FFN_DOWN_BLOCKS)]
        h, (wg, wu, wd) = _mix_call(h, y_attn, w_rest, vecs, w_s, bsb, pa, pb, pc, wo, l, seq, jobs)
        jobs = qkv_jobs(l + 1, n_cast) if l + 1 < depth else []
        h, w_next = _ffn_call(h, wg, wu, wd, vecs, l, jobs)
        w_groups = tuple(w_next)
    return h.reshape(batch, seq, D_MODEL)
```

```python
import functools

import numpy as np
import jax
import jax.numpy as jnp
from jax import lax
from jax.experimental import pallas as pl
from jax.experimental.pallas import tpu as pltpu

F32 = jnp.float32
BF16 = jnp.bfloat16

D_MODEL = 1024
HEADS = 8
HEAD_DIM = 64
HALF = HEAD_DIM // 2
GROUP_COL = HEADS * HEAD_DIM
DILATIONS = (1, 4, 16)
N_GROUPS = len(DILATIONS)
ATTN_BLK = 128
N_RES = 16
CHUNK = N_RES * ATTN_BLK
STEP_BLOCKS = 8
_STEPS_PER_GROUP = N_RES // STEP_BLOCKS
GMLP_GROUPS = 8
GMLP_CHUNK = 128
D_FF = 2816
DEPTH = 2
ALPHA = (2 * DEPTH) ** 0.25
LN_EPS = 1e-5
ROPE_THETA = 10000.0
LOG2_E = 1.4426950408889634

TILE = 512
QKV_SUB = 2
FFN_ROWS = 1024
FFN_CHUNKS = 1
ROWS_PER_RES = TILE // N_RES
LANES = 128
SUBLANES = 8
NEG = -1e30
VMEM_LIMIT = 56 * 1024 * 1024

_OFF_ATTN = 6 * D_MODEL
_QKV_COLS = 3 * GROUP_COL
_OFF_GMLP = _OFF_ATTN + N_GROUPS * _QKV_COLS
_REST_COLS = _OFF_ATTN + 2 * D_MODEL
_CAST_BLOCKS = 16
_FFN_DOWN_BLOCKS = _CAST_BLOCKS

_P_CONV, _P_GLN_G, _P_GLN_B, _P_LN1_G, _P_LN1_B, _P_LN2_G, _P_LN2_B = 0, 3, 4, 5, 6, 7, 8
_P_ROWS = 9


def _resident(shape):
    nd = len(shape)
    return pl.BlockSpec(shape, lambda *_: (0,) * nd, pipeline_mode=pl.Buffered(1))


def _layer_resident(shape, layer):
    nd = len(shape)
    return pl.BlockSpec((None,) + tuple(shape[1:]), lambda *_: (layer,) + (0,) * (nd - 1),
                        pipeline_mode=pl.Buffered(1))


def _layer_norm(v, g, b):
    mu = jnp.mean(v, axis=-1, keepdims=True)
    c = v - mu
    var = jnp.mean(c * c, axis=-1, keepdims=True)
    return c * lax.rsqrt(var + LN_EPS) * g + b


def _gelu(v):
    return 0.5 * v * (1.0 + lax.erf(v * 0.7071067811865476))


class _CastJob:
    def __init__(self, src, layer, rows, cols, n_blocks, *, col_block=0, by_cols=False, col_map=None):
        self.src, self.n_blocks = src, n_blocks
        last = n_blocks - 1
        if by_cols:
            bw = cols // n_blocks
            self.in_spec = pl.BlockSpec((None, rows, bw), lambda i: (layer, 0, col_map(jnp.minimum(i, last))))
            self.out_spec = pl.BlockSpec((rows, bw), lambda i: (0, jnp.minimum(i, last)))
        else:
            rb = rows // n_blocks
            self.in_spec = pl.BlockSpec((None, rb, cols), lambda i: (layer, jnp.minimum(i, last), col_block))
            self.out_spec = pl.BlockSpec((rb, cols), lambda i: (jnp.minimum(i, last), 0))
        self.out_shape = jax.ShapeDtypeStruct((rows, cols), BF16)


def _run_casts(jobs, src_refs, dst_refs, step, n_steps):
    for job, s, d in zip(jobs, src_refs, dst_refs):
        if job.n_blocks >= n_steps:
            d[...] = s[...].astype(BF16)
        else:
            @pl.when(step < job.n_blocks)
            def _(s=s, d=d):
                d[...] = s[...].astype(BF16)


def _cast_kernel(*refs):
    n = len(refs) // 2
    for s, d in zip(refs[:n], refs[n:]):
        d[...] = s[...].astype(BF16)


def _cast_call(jobs, n_steps):
    return pl.pallas_call(
        _cast_kernel,
        out_shape=tuple(j.out_shape for j in jobs),
        grid=(n_steps,),
        in_specs=[j.in_spec for j in jobs],
        out_specs=tuple(j.out_spec for j in jobs),
        compiler_params=pltpu.CompilerParams(dimension_semantics=("arbitrary",)),
        name="cast_weights",
    )(*[j.src for j in jobs])


def _rope_tables(pos4, inv_row):
    ang = pos4 * inv_row
    lane = lax.broadcasted_iota(jnp.int32, (TILE // 4, LANES), 1)
    group = lane // HALF
    sign = jnp.where((lane % HEAD_DIM) < HALF, -1.0, 1.0)
    out = []
    for dense, sgn in ((jnp.cos(ang), None), (jnp.sin(ang), sign)):
        rolled = [dense] + [pltpu.roll(dense, HALF * k, 1) for k in range(1, 4)]
        slabs = []
        for a in range(4):
            e = rolled[(0 - a) % 4]
            for g in range(1, 4):
                e = jnp.where(group == g, rolled[(g - a) % 4], e)
            slabs.append(e if sgn is None else e * sgn)
        out.append(jnp.concatenate(slabs, axis=0))
    return out


def _qkv_kernel(*refs, n_casts, n_steps, jobs):
    x_refs = refs[:8]
    w_refs = refs[8:11]
    pos_ref, inv_ref = refs[11:13]
    cast_src = refs[13:13 + n_casts]
    out_refs = refs[13 + n_casts:22 + n_casts]
    cast_dst = refs[22 + n_casts:]
    _run_casts(jobs, cast_src, cast_dst, pl.program_id(0), n_steps)

    scale = HEAD_DIM ** -0.5 * LOG2_E
    first_half = (lax.broadcasted_iota(jnp.int32, (TILE, LANES), 1) % HEAD_DIM) < HALF

    for h in range(QKV_SUB):
        rows = [jnp.concatenate([xr[pl.ds(TILE * h + rho, ROWS_PER_RES, stride=N_RES), :] for xr in x_refs], axis=1)
                for rho in range(N_RES)]
        xp = jnp.concatenate(rows, axis=0).astype(BF16)
        cos, sin = _rope_tables(pos_ref[TILE // 4 * h:TILE // 4 * (h + 1), :], inv_ref[...])
        tables = ((cos * scale, sin * scale), (cos, sin))

        for g in range(N_GROUPS):
            for j in range(3):
                y = jnp.dot(xp, w_refs[g][:, GROUP_COL * j:GROUP_COL * (j + 1)], preferred_element_type=F32)
                if j < 2:
                    c, s = tables[j]
                    parts = []
                    for sl in range(GROUP_COL // LANES):
                        a = y[:, LANES * sl:LANES * (sl + 1)]
                        partner = jnp.where(first_half, pltpu.roll(a, LANES - HALF, 1), pltpu.roll(a, HALF, 1))
                        parts.append(a * c + partner * s)
                    y = jnp.concatenate(parts, axis=1)
                o_ref = out_refs[3 * g + j]
                if g == 0:
                    pieces = [y[32 * rho + 8 * c:32 * rho + 8 * c + 8, :] for c in range(4) for rho in range(N_RES)]
                    o_ref[TILE * h:TILE * (h + 1), :] = jnp.concatenate(pieces, axis=0).astype(BF16)
                elif g == 1:
                    for a in range(4):
                        slab = jnp.concatenate([y[32 * (4 * b + a):32 * (4 * b + a) + 32, :] for b in range(4)],
                                               axis=0)
                        o_ref[ATTN_BLK * h:ATTN_BLK * (h + 1), GROUP_COL * a:GROUP_COL * (a + 1)] = slab.astype(BF16)
                else:
                    for rho in range(N_RES):
                        o_ref[ROWS_PER_RES * h:ROWS_PER_RES * (h + 1), GROUP_COL * rho:GROUP_COL * (rho + 1)] = (
                            y[32 * rho:32 * rho + 32, :].astype(BF16))


def _qkv_call(x2d, w_groups, pos4, inv_row, jobs):
    t = x2d.shape[0]
    step_rows = QKV_SUB * TILE
    nt = t // step_rows
    x_specs = [pl.BlockSpec((step_rows, LANES), functools.partial(lambda i, c: (i, c), c=c)) for c in range(8)]
    out_shapes, out_specs = [], []
    for d in DILATIONS:
        for _ in range(3):
            out_shapes.append(jax.ShapeDtypeStruct((t // d, d * GROUP_COL), BF16))
            out_specs.append(pl.BlockSpec((step_rows // d, d * GROUP_COL), lambda i: (i, 0)))
    outs = pl.pallas_call(
        functools.partial(_qkv_kernel, n_casts=len(jobs), n_steps=nt, jobs=jobs),
        out_shape=tuple(out_shapes) + tuple(j.out_shape for j in jobs),
        grid=(nt,),
        in_specs=x_specs + [_resident(w.shape) for w in w_groups]
                 + [pl.BlockSpec((step_rows // 4, LANES), lambda i: (i, 0)), _resident(inv_row.shape)]
                 + [j.in_spec for j in jobs],
        out_specs=tuple(out_specs) + tuple(j.out_spec for j in jobs),
        compiler_params=pltpu.CompilerParams(dimension_semantics=("arbitrary",), vmem_limit_bytes=VMEM_LIMIT),
        name="qkv_rope",
    )(*([x2d] * 8), *w_groups, pos4, inv_row, *[j.src for j in jobs])
    return outs[:9], outs[9:]


def _mask16(cond):
    return jnp.where(cond, 1.0, 0.0).astype(BF16) > 0


def _attn_block(q, keys_of, vals_of, bias_t, state_get, state_put, emit):
    lo16 = _mask16(lax.broadcasted_iota(jnp.int32, (ATTN_BLK, LANES), 1) < HEAD_DIM)
    lo = lax.broadcasted_iota(jnp.int32, (1, LANES), 1) < HEAD_DIM
    eye = jnp.where(lax.broadcasted_iota(jnp.int32, (2 * ATTN_BLK, LANES), 0) % ATTN_BLK
                    == lax.broadcasted_iota(jnp.int32, (2 * ATTN_BLK, LANES), 1), 1.0, 0.0).astype(BF16)
    for j in range(HEADS // 2):
        sl = slice(LANES * j, LANES * (j + 1))
        qj = q[:, sl]
        zero = jnp.zeros_like(qj)
        lhs = jnp.concatenate([jnp.where(lo16, qj, zero), jnp.where(lo16, zero, qj)], axis=0)
        scores = lax.dot_general(jnp.concatenate([lhs, eye], axis=1),
                                 jnp.concatenate([keys_of(sl), bias_t], axis=1),
                                 (((1,), (1,)), ((), ())), preferred_element_type=F32)
        vals = vals_of(sl)
        rhs = jnp.concatenate([vals, jnp.ones_like(vals)], axis=1)
        outs, maxes = [], []
        for hl in range(2):
            s = scores[ATTN_BLK * hl:ATTN_BLK * (hl + 1), :]
            mb = jnp.max(s, axis=1, keepdims=True)
            p = jnp.exp2(s - mb).astype(BF16)
            outs.append(jnp.dot(p, rhs, preferred_element_type=F32))
            maxes.append(mb)
        o_new = jnp.where(lo, outs[0][:, :LANES], outs[1][:, :LANES])
        l_new = jnp.where(lo, outs[0][:, LANES:], outs[1][:, LANES:])
        m_new = jnp.where(lo, maxes[0], maxes[1])
        old = state_get(j)
        if old is not None:
            acc0, l0, m0 = old
            m = jnp.maximum(m0, m_new)
            a0 = jnp.exp2(m0 - m)
            a1 = jnp.exp2(m_new - m)
            o_new = acc0 * a0 + o_new * a1
            l_new = l0 * a0 + l_new * a1
            m_new = m
        if emit is not None:
            emit(j, o_new / l_new)
        else:
            state_put(j, o_new, l_new, m_new)


def _attn_kernel(q1, k1, v1, kp1, vp1, q2, k2, v2, kp2, vp2, q3, k3, v3, kp3, vp3, mask_ref, out_ref,
                 acc_s, l_s, m_s):
    n = pl.program_id(1)
    s = pl.program_id(2)
    state = (acc_s, l_s, m_s)

    def lanes(j):
        return slice(LANES * j, LANES * (j + 1))

    def bias_of(g, first):
        return mask_ref[g, 0] if first is None else mask_ref[g, first.astype(jnp.int32)]

    def two_blocks(prev_ref, cur_ref, col0):
        def load(sl):
            cols = slice(col0 + sl.start, col0 + sl.stop)
            return jnp.concatenate([prev_ref[:, cols], cur_ref[:ATTN_BLK, cols]], axis=0)
        return load

    spg = _STEPS_PER_GROUP

    @pl.when(s < spg)
    def _group1():
        for c in range(STEP_BLOCKS):
            k_blk = pl.multiple_of((STEP_BLOCKS * s + c) * SUBLANES, SUBLANES)

            def put(j, *vals, k_blk=k_blk):
                for ref, val in zip(state, vals):
                    for rho in range(N_RES):
                        ref[rho, pl.ds(k_blk, SUBLANES), lanes(j)] = val[SUBLANES * rho:SUBLANES * (rho + 1), :]

            rows = slice(ATTN_BLK * c, ATTN_BLK * (c + 1))
            if c == 0:
                keys_of = lambda sl: jnp.concatenate([kp1[:, sl], k1[:ATTN_BLK, sl]], axis=0)
                vals_of = lambda sl: jnp.concatenate([vp1[:, sl], v1[:ATTN_BLK, sl]], axis=0)
                first = jnp.logical_and(n == 0, s == 0)
            else:
                both = slice(ATTN_BLK * (c - 1), ATTN_BLK * (c + 1))
                keys_of = lambda sl, both=both: k1[both, sl]
                vals_of = lambda sl, both=both: v1[both, sl]
                first = None
            _attn_block(q1[rows, :], keys_of, vals_of, bias_of(0, first), lambda j: None, put, None)

    @pl.when(jnp.logical_and(s >= spg, s < 2 * spg))
    def _group2():
        t = s - spg
        for kk in range(STEP_BLOCKS // 4):
            base = pl.multiple_of((t * (STEP_BLOCKS // 4) + kk) * 32, 32)
            bias = bias_of(1, jnp.logical_and(n == 0, t == 0)) if kk == 0 else bias_of(1, None)
            rows = slice(ATTN_BLK * kk, ATTN_BLK * (kk + 1))
            both = slice(ATTN_BLK * (kk - 1), ATTN_BLK * (kk + 1))
            for a in range(4):
                def get(j, a=a, base=base):
                    return tuple(jnp.concatenate([ref[4 * b + a, pl.ds(base, 32), lanes(j)] for b in range(4)],
                                                 axis=0) for ref in state)

                def put(j, *vals, a=a, base=base):
                    for ref, val in zip(state, vals):
                        for b in range(4):
                            ref[4 * b + a, pl.ds(base, 32), lanes(j)] = val[32 * b:32 * (b + 1), :]

                col0 = GROUP_COL * a
                if kk == 0:
                    keys_of = two_blocks(kp2, k2, col0)
                    vals_of = two_blocks(vp2, v2, col0)
                else:
                    keys_of = lambda sl, col0=col0, both=both: k2[both, col0 + sl.start:col0 + sl.stop]
                    vals_of = lambda sl, col0=col0, both=both: v2[both, col0 + sl.start:col0 + sl.stop]
                _attn_block(q2[rows, col0:col0 + GROUP_COL], keys_of, vals_of, bias, get, put, None)

    @pl.when(s >= 2 * spg)
    def _group3():
        t = s - 2 * spg
        bias = bias_of(2, n == 0)
        for c in range(STEP_BLOCKS):
            rho = STEP_BLOCKS * t + c

            def get(j, rho=rho):
                return tuple(ref[rho, :, lanes(j)] for ref in state)

            def emit(j, y, c=c):
                out_ref[0, c, :, lanes(j)] = y.astype(BF16)

            col0 = GROUP_COL * c
            _attn_block(q3[:, col0:col0 + GROUP_COL], two_blocks(kp3, k3, col0), two_blocks(vp3, v3, col0),
                        bias, get, None, emit)


def _attn_call(qkv, masks, batch, seq):
    nchunk = seq // CHUNK
    spg = _STEPS_PER_GROUP
    rows1 = STEP_BLOCKS * ATTN_BLK
    rows2 = (STEP_BLOCKS // 4) * ATTN_BLK

    def step(s, g):
        return jnp.clip(s - spg * g, 0, spg - 1)

    def cur1(b, n, s):
        return (b * (seq // rows1) + spg * n + step(s, 0), 0)

    def prev1(b, n, s):
        return (jnp.maximum(STEP_BLOCKS * cur1(b, n, s)[0] - 1, 0), 0)

    def cur2(b, n, s):
        return (b * (seq // 4 // rows2) + spg * n + step(s, 1), 0)

    def prev2(b, n, s):
        return (jnp.maximum((STEP_BLOCKS // 4) * cur2(b, n, s)[0] - 1, 0), 0)

    def cur3(b, n, s):
        return (b * nchunk + n, step(s, 2))

    def prev3(b, n, s):
        return (jnp.maximum(b * nchunk + n - 1, 0), step(s, 2))

    blk1 = (rows1, GROUP_COL)
    blk2 = (rows2, 4 * GROUP_COL)
    prev2_blk = (ATTN_BLK, 4 * GROUP_COL)
    blk3 = (ATTN_BLK, STEP_BLOCKS * GROUP_COL)
    in_specs = [
        pl.BlockSpec(blk1, cur1), pl.BlockSpec(blk1, cur1), pl.BlockSpec(blk1, cur1),
        pl.BlockSpec((ATTN_BLK, GROUP_COL), prev1), pl.BlockSpec((ATTN_BLK, GROUP_COL), prev1),
        pl.BlockSpec(blk2, cur2), pl.BlockSpec(blk2, cur2), pl.BlockSpec(blk2, cur2),
        pl.BlockSpec(prev2_blk, prev2), pl.BlockSpec(prev2_blk, prev2),
        pl.BlockSpec(blk3, cur3), pl.BlockSpec(blk3, cur3), pl.BlockSpec(blk3, cur3),
        pl.BlockSpec(blk3, prev3), pl.BlockSpec(blk3, prev3),
        _resident(masks.shape),
    ]
    q1, k1, v1, q2, k2, v2, q3, k3, v3 = qkv
    return pl.pallas_call(
        _attn_kernel,
        out_shape=jax.ShapeDtypeStruct((batch, N_RES, seq // N_RES, GROUP_COL), BF16),
        grid=(batch, nchunk, N_GROUPS * spg),
        in_specs=in_specs,
        out_specs=pl.BlockSpec((1, STEP_BLOCKS, ATTN_BLK, GROUP_COL), lambda b, n, s: (b, step(s, 2), n, 0)),
        scratch_shapes=[pltpu.VMEM((N_RES, ATTN_BLK, GROUP_COL), F32)] * 3,
        compiler_params=pltpu.CompilerParams(dimension_semantics=("arbitrary",) * 3, vmem_limit_bytes=VMEM_LIMIT),
        name="dilated_attention",
    )(q1, k1, v1, k1, v1, q2, k2, v2, k2, v2, q3, k3, v3, k3, v3, masks)


def _mix_kernel(*refs, tiles_per_batch, n_casts, n_steps, jobs):
    x_ref, y_ref, w_ref, vec_ref, ws_ref, bsb_ref, pa_ref, pb_ref, pc_ref, wo_ref = refs[:10]
    cast_src = refs[10:10 + n_casts]
    o_ref = refs[10 + n_casts]
    cast_dst = refs[11 + n_casts:11 + 2 * n_casts]
    zc_ref, ys_ref = refs[11 + 2 * n_casts:]
    i = pl.program_id(0)
    _run_casts(jobs, cast_src, cast_dst, i, n_steps)

    x = x_ref[...]
    xb = x.astype(BF16)

    def vec(r):
        return vec_ref[r:r + 1, :]

    def proj(k):
        return jnp.dot(xb, w_ref[:, D_MODEL * k:D_MODEL * (k + 1)], preferred_element_type=F32)

    @pl.when(i % tiles_per_batch == 0)
    def _():
        zc_ref[...] = jnp.zeros_like(zc_ref)

    z = proj(4) * proj(5)
    head = jnp.concatenate([zc_ref[...], z[:SUBLANES, :]], axis=0)
    zc_ref[...] = z[TILE - SUBLANES:, :]
    conv = vec(_P_CONV + 2) * z
    for shift in (1, 2):
        zs = jnp.concatenate([pltpu.roll(head, shift, 0)[SUBLANES:, :], pltpu.roll(z, shift, 0)[SUBLANES:, :]], axis=0)
        conv = conv + vec(_P_CONV + 2 - shift) * zs
    ya = (proj(3) * conv).astype(BF16)
    m = jax.nn.sigmoid(proj(0)) * jnp.dot(ya, pa_ref[...], preferred_element_type=F32)

    u = _gelu(proj(6))
    vb = _layer_norm(_gelu(proj(7)), vec(_P_GLN_G), vec(_P_GLN_B)).astype(BF16)
    n_chunks = TILE // GMLP_CHUNK
    tril = (lax.broadcasted_iota(jnp.int32, (GMLP_CHUNK, GMLP_CHUNK), 0)
            >= lax.broadcasted_iota(jnp.int32, (GMLP_CHUNK, GMLP_CHUNK), 1))
    cols = []
    for g in range(GMLP_GROUPS):
        gl = slice(LANES * g, LANES * (g + 1))
        wg = jnp.where(tril, ws_ref[g], 0.0).astype(BF16)
        rhs = jnp.concatenate([vb[GMLP_CHUNK * c:GMLP_CHUNK * (c + 1), gl] for c in range(n_chunks)], axis=1)
        sp = jnp.dot(wg, rhs, preferred_element_type=F32)
        bias = bsb_ref[:, gl]
        cols.append(jnp.concatenate([sp[:, LANES * c:LANES * (c + 1)] + bias for c in range(n_chunks)], axis=0))
    yc = (u * jnp.concatenate(cols, axis=1)).astype(BF16)
    m = m + jax.nn.sigmoid(proj(2)) * jnp.dot(yc, pc_ref[...], preferred_element_type=F32)

    for rho in range(N_RES):
        piece = y_ref[0, rho].astype(F32)
        for sl in range(GROUP_COL // LANES):
            ys_ref[sl, pl.ds(rho, ROWS_PER_RES, stride=N_RES), :] = piece[:, LANES * sl:LANES * (sl + 1)]
    yb = jnp.concatenate([ys_ref[sl] for sl in range(GROUP_COL // LANES)], axis=1).astype(BF16)
    m = m + jax.nn.sigmoid(proj(1)) * jnp.dot(yb, pb_ref[...], preferred_element_type=F32)

    mixed = jnp.dot(m.astype(BF16), wo_ref[...], preferred_element_type=F32)
    o_ref[...] = _layer_norm(ALPHA * x + mixed, vec(_P_LN1_G), vec(_P_LN1_B))


def _mix_call(x2d, y_attn, w_rest, vecs, w_s, bsb, pa, pb, pc, wo, layer, seq, jobs):
    t = x2d.shape[0]
    nt = t // TILE
    tiles_per_batch = seq // TILE
    outs = pl.pallas_call(
        functools.partial(_mix_kernel, tiles_per_batch=tiles_per_batch, n_casts=len(jobs), n_steps=nt, jobs=jobs),
        out_shape=(jax.ShapeDtypeStruct((t, D_MODEL), F32),) + tuple(j.out_shape for j in jobs),
        grid=(nt,),
        in_specs=[pl.BlockSpec((TILE, D_MODEL), lambda i: (i, 0)),
                  pl.BlockSpec((1, N_RES, ROWS_PER_RES, GROUP_COL),
                               lambda i: (i // tiles_per_batch, 0, i % tiles_per_batch, 0)),
                  _resident(w_rest.shape), _layer_resident(vecs.shape, layer), _layer_resident(w_s.shape, layer),
                  _layer_resident(bsb.shape, layer)]
                 + [_resident(w.shape) for w in (pa, pb, pc, wo)] + [j.in_spec for j in jobs],
        out_specs=(pl.BlockSpec((TILE, D_MODEL), lambda i: (i, 0)),) + tuple(j.out_spec for j in jobs),
        scratch_shapes=[pltpu.VMEM((SUBLANES, D_MODEL), F32),
                        pltpu.VMEM((GROUP_COL // LANES, TILE, LANES), F32)],
        compiler_params=pltpu.CompilerParams(dimension_semantics=("arbitrary",), vmem_limit_bytes=VMEM_LIMIT),
        name="mixers_out_ln",
    )(x2d, y_attn, w_rest, vecs, w_s, bsb, pa, pb, pc, wo, *[j.src for j in jobs])
    return outs[0], outs[1:]


def _ffn_kernel(*refs, n_casts, n_steps, jobs):
    x_ref, wg_ref, wu_ref, wd_ref, vec_ref = refs[:5]
    cast_src = refs[5:5 + n_casts]
    o_ref = refs[5 + n_casts]
    cast_dst = refs[6 + n_casts:]
    _run_casts(jobs, cast_src, cast_dst, pl.program_id(0), n_steps)

    x = x_ref[...]
    xb = x.astype(BF16)
    down = None
    for c in range(FFN_CHUNKS):
        cs = slice(D_FF // FFN_CHUNKS * c, D_FF // FFN_CHUNKS * (c + 1))
        gate = jnp.dot(xb, wg_ref[:, cs], preferred_element_type=F32)
        up = jnp.dot(xb, wu_ref[:, cs], preferred_element_type=F32)
        h = (jax.nn.silu(gate) * up).astype(BF16)
        part = jnp.dot(h, wd_ref[cs, :], preferred_element_type=F32)
        down = part if down is None else down + part
    o_ref[...] = _layer_norm(ALPHA * x + down, vec_ref[_P_LN2_G:_P_LN2_G + 1, :], vec_ref[_P_LN2_B:_P_LN2_B + 1, :])


def _ffn_call(x2d, wg, wu, wd, vecs, layer, jobs):
    t = x2d.shape[0]
    nt = t // FFN_ROWS
    outs = pl.pallas_call(
        functools.partial(_ffn_kernel, n_casts=len(jobs), n_steps=nt, jobs=jobs),
        out_shape=(jax.ShapeDtypeStruct((t, D_MODEL), F32),) + tuple(j.out_shape for j in jobs),
        grid=(nt,),
        in_specs=[pl.BlockSpec((FFN_ROWS, D_MODEL), lambda i: (i, 0))] + [_resident(w.shape) for w in (wg, wu, wd)]
                 + [_layer_resident(vecs.shape, layer)] + [j.in_spec for j in jobs],
        out_specs=(pl.BlockSpec((FFN_ROWS, D_MODEL), lambda i: (i, 0)),) + tuple(j.out_spec for j in jobs),
        compiler_params=pltpu.CompilerParams(dimension_semantics=("arbitrary",), vmem_limit_bytes=VMEM_LIMIT),
        name="swiglu_ln",
    )(x2d, wg, wu, wd, vecs, *[j.src for j in jobs])
    return outs[0], outs[1:]


def _band_masks():
    u = np.arange(ATTN_BLK)
    stored = (
        (u % N_RES) * SUBLANES + u // N_RES,
        (u % 4) * 32 + u // 4,
        u,
    )
    out = np.empty((N_GROUPS, 2, 2 * ATTN_BLK, ATTN_BLK), np.float32)
    for g, st in enumerate(stored):
        orig = np.empty(ATTN_BLK, np.int64)
        orig[st] = u
        ki = orig[:, None]
        qi = orig[None, :]
        out[g, 0, :ATTN_BLK] = np.where(ki >= qi, 0.0, NEG)
        out[g, 1, :ATTN_BLK] = NEG
        out[g, :, ATTN_BLK:] = np.where(ki <= qi, 0.0, NEG)
    return out


def kernel(x, positions, w_in, conv_w, gmlp_ln_g, gmlp_ln_b, w_s, b_s, p_a, p_b, p_c, w_o, ln1_g, ln1_b, w_gate, w_up,
           w_down, ln2_g, ln2_b):
    batch, seq, d = x.shape
    t = batch * seq
    nt = t // TILE
    depth = w_in.shape[0]
    n_cast = _CAST_BLOCKS
    assert d == D_MODEL and seq % CHUNK == 0 and depth == DEPTH
    assert min(t // (QKV_SUB * TILE), t // FFN_ROWS, nt) >= n_cast

    pos = positions.reshape(nt, ROWS_PER_RES, N_RES).transpose(0, 2, 1).reshape(nt, 4, TILE // 4).astype(F32)
    pos4 = jnp.repeat(pos.transpose(0, 2, 1), HALF, axis=2).reshape(t // 4, LANES)
    inv_freq = ROPE_THETA ** (-jnp.arange(HALF, dtype=F32) / HALF)
    inv_row = jnp.tile(inv_freq, LANES // HALF).reshape(1, LANES)

    masks = jnp.asarray(_band_masks(), dtype=BF16)
    vecs = jnp.concatenate([conv_w] + [v[:, None, :] for v in (gmlp_ln_g, gmlp_ln_b, ln1_g, ln1_b, ln2_g, ln2_b)],
                           axis=1)
    bsb = jnp.repeat(jnp.swapaxes(b_s, 1, 2), GMLP_CHUNK, axis=2)

    rest_bw = _REST_COLS // n_cast
    gap = (_OFF_GMLP - _OFF_ATTN) // rest_bw

    def rest_col_block(i):
        return jnp.where(i < _OFF_ATTN // rest_bw, i, i + gap)

    def qkv_jobs(layer, n_blocks):
        return [_CastJob(w_in, layer, D_MODEL, _QKV_COLS, n_blocks, col_block=_OFF_ATTN // _QKV_COLS + g)
                for g in range(N_GROUPS)]

    w_groups = _cast_call(qkv_jobs(0, SUBLANES), SUBLANES)
    h = x.reshape(t, D_MODEL)
    for l in range(depth):
        jobs = [_CastJob(w_in, l, D_MODEL, _REST_COLS, n_cast, by_cols=True, col_map=rest_col_block),
                _CastJob(p_a, l, D_MODEL, D_MODEL, n_cast), _CastJob(p_b, l, GROUP_COL, D_MODEL, n_cast),
                _CastJob(p_c, l, D_MODEL, D_MODEL, n_cast), _CastJob(w_o, l, D_MODEL, D_MODEL, n_cast)]
        qkv, (w_rest, pa, pb, pc, wo) = _qkv_call(h, w_groups, pos4, inv_row, jobs)
        y_attn = _attn_call(qkv, masks, batch, seq)
        jobs = [_CastJob(w_gate, l, D_MODEL, D_FF, n_cast), _CastJob(w_up, l, D_MODEL, D_FF, n_cast),
                _CastJob(w_down, l, D_FF, D_MODEL, _FFN_DOWN_BLOCKS)]
        h, (wg, wu, wd) = _mix_call(h, y_attn, w_rest, vecs, w_s, bsb, pa, pb, pc, wo, l, seq, jobs)
        jobs = qkv_jobs(l + 1, n_cast) if l + 1 < depth else []
        h, w_next = _ffn_call(h, wg, wu, wd, vecs, l, jobs)
        w_groups = tuple(w_next)
    return h.reshape(batch, seq, D_MODEL)
```

```python
import functools

import numpy as np
import jax
import jax.numpy as jnp
from jax import lax
from jax.experimental import pallas as pl
from jax.experimental.pallas import tpu as pltpu

F32 = jnp.float32
BF16 = jnp.bfloat16

D_MODEL = 1024
HEADS = 8
HEAD_DIM = 64
HALF = HEAD_DIM // 2
GROUP_COL = HEADS * HEAD_DIM
DILATIONS = (1, 4, 16)
N_GROUPS = len(DILATIONS)
ATTN_BLK = 128
N_RES = 16
CHUNK = N_RES * ATTN_BLK
STEP_BLOCKS = 8
_STEPS_PER_GROUP = N_RES // STEP_BLOCKS
GMLP_GROUPS = 8
GMLP_CHUNK = 128
D_FF = 2816
DEPTH = 2
ALPHA = (2 * DEPTH) ** 0.25
LN_EPS = 1e-5
ROPE_THETA = 10000.0
LOG2_E = 1.4426950408889634

TILE = 512
QKV_SUB = 2
MIX_ROWS = 1024
MIX_SLAB = 256
FFN_ROWS = 1024
ROWS_PER_RES = TILE // N_RES
LANES = 128
SUBLANES = 8
NEG = -1e30
VMEM_LIMIT = 56 * 1024 * 1024

_OFF_ATTN = 6 * D_MODEL
_QKV_COLS = 3 * GROUP_COL
_OFF_GMLP = _OFF_ATTN + N_GROUPS * _QKV_COLS
_REST_COLS = _OFF_ATTN + 2 * D_MODEL
_CAST_BLOCKS = 16

_P_CONV, _P_GLN_G, _P_GLN_B, _P_LN1_G, _P_LN1_B, _P_LN2_G, _P_LN2_B = 0, 3, 4, 5, 6, 7, 8
_P_ROWS = 9


def _resident(shape):
    nd = len(shape)
    return pl.BlockSpec(shape, lambda *_: (0,) * nd, pipeline_mode=pl.Buffered(1))


def _layer_resident(shape, layer):
    nd = len(shape)
    return pl.BlockSpec((None,) + tuple(shape[1:]), lambda *_: (layer,) + (0,) * (nd - 1),
                        pipeline_mode=pl.Buffered(1))


def _layer_norm(v, g, b):
    mu = jnp.mean(v, axis=-1, keepdims=True)
    c = v - mu
    var = jnp.mean(c * c, axis=-1, keepdims=True)
    return c * lax.rsqrt(var + LN_EPS) * g + b


def _gelu(v):
    return 0.5 * v * (1.0 + lax.erf(v * 0.7071067811865476))


class _CastJob:
    def __init__(self, src, layer, rows, cols, n_blocks, *, col_block=0, by_cols=False, col_map=None,
                 step_of=lambda i: i):
        self.src, self.n_blocks = src, n_blocks

        def blk(*grid):
            return jnp.minimum(step_of(*grid), n_blocks - 1)

        if by_cols:
            bw = cols // n_blocks
            self.in_spec = pl.BlockSpec((None, rows, bw), lambda *g: (layer, 0, col_map(blk(*g))))
            self.out_spec = pl.BlockSpec((rows, bw), lambda *g: (0, blk(*g)))
        else:
            rb = rows // n_blocks
            self.in_spec = pl.BlockSpec((None, rb, cols), lambda *g: (layer, blk(*g), col_block))
            self.out_spec = pl.BlockSpec((rb, cols), lambda *g: (blk(*g), 0))
        self.out_shape = jax.ShapeDtypeStruct((rows, cols), BF16)


def _run_casts(jobs, src_refs, dst_refs, step, n_steps):
    for job, s, d in zip(jobs, src_refs, dst_refs):
        if job.n_blocks >= n_steps:
            d[...] = s[...].astype(BF16)
        else:
            @pl.when(step < job.n_blocks)
            def _(s=s, d=d):
                d[...] = s[...].astype(BF16)


def _cast_kernel(*refs):
    n = len(refs) // 2
    for s, d in zip(refs[:n], refs[n:]):
        d[...] = s[...].astype(BF16)


def _cast_call(jobs, n_steps):
    return pl.pallas_call(
        _cast_kernel,
        out_shape=tuple(j.out_shape for j in jobs),
        grid=(n_steps,),
        in_specs=[j.in_spec for j in jobs],
        out_specs=tuple(j.out_spec for j in jobs),
        compiler_params=pltpu.CompilerParams(dimension_semantics=("arbitrary",)),
        name="cast_weights",
    )(*[j.src for j in jobs])


def _rope_tables(pos4, inv_row):
    ang = pos4 * inv_row
    lane = lax.broadcasted_iota(jnp.int32, (TILE // 4, LANES), 1)
    group = lane // HALF
    sign = jnp.where((lane % HEAD_DIM) < HALF, -1.0, 1.0)
    out = []
    for dense, sgn in ((jnp.cos(ang), None), (jnp.sin(ang), sign)):
        rolled = [dense] + [pltpu.roll(dense, HALF * k, 1) for k in range(1, 4)]
        slabs = []
        for a in range(4):
            e = rolled[(0 - a) % 4]
            for g in range(1, 4):
                e = jnp.where(group == g, rolled[(g - a) % 4], e)
            slabs.append(e if sgn is None else e * sgn)
        out.append(jnp.concatenate(slabs, axis=0))
    return out


def _qkv_kernel(*refs, n_casts, n_steps, jobs):
    x_refs = refs[:8]
    w_refs = refs[8:11]
    pos_ref, inv_ref = refs[11:13]
    cast_src = refs[13:13 + n_casts]
    out_refs = refs[13 + n_casts:22 + n_casts]
    cast_dst = refs[22 + n_casts:]
    _run_casts(jobs, cast_src, cast_dst, pl.program_id(0), n_steps)

    scale = HEAD_DIM ** -0.5 * LOG2_E
    first_half = (lax.broadcasted_iota(jnp.int32, (TILE, LANES), 1) % HEAD_DIM) < HALF

    for h in range(QKV_SUB):
        rows = [jnp.concatenate([xr[pl.ds(TILE * h + rho, ROWS_PER_RES, stride=N_RES), :] for xr in x_refs], axis=1)
                for rho in range(N_RES)]
        xp = jnp.concatenate(rows, axis=0).astype(BF16)
        cos, sin = _rope_tables(pos_ref[TILE // 4 * h:TILE // 4 * (h + 1), :], inv_ref[...])
        tables = ((cos * scale, sin * scale), (cos, sin))

        for g in range(N_GROUPS):
            for j in range(3):
                y = jnp.dot(xp, w_refs[g][:, GROUP_COL * j:GROUP_COL * (j + 1)], preferred_element_type=F32)
                if j < 2:
                    c, s = tables[j]
                    parts = []
                    for sl in range(GROUP_COL // LANES):
                        a = y[:, LANES * sl:LANES * (sl + 1)]
                        partner = jnp.where(first_half, pltpu.roll(a, LANES - HALF, 1), pltpu.roll(a, HALF, 1))
                        parts.append(a * c + partner * s)
                    y = jnp.concatenate(parts, axis=1)
                o_ref = out_refs[3 * g + j]
                if g == 0:
                    pieces = [y[32 * rho + 8 * c:32 * rho + 8 * c + 8, :] for c in range(4) for rho in range(N_RES)]
                    o_ref[TILE * h:TILE * (h + 1), :] = jnp.concatenate(pieces, axis=0).astype(BF16)
                elif g == 1:
                    for a in range(4):
                        slab = jnp.concatenate([y[32 * (4 * b + a):32 * (4 * b + a) + 32, :] for b in range(4)],
                                               axis=0)
                        o_ref[ATTN_BLK * h:ATTN_BLK * (h + 1), GROUP_COL * a:GROUP_COL * (a + 1)] = slab.astype(BF16)
                else:
                    for rho in range(N_RES):
                        o_ref[ROWS_PER_RES * h:ROWS_PER_RES * (h + 1), GROUP_COL * rho:GROUP_COL * (rho + 1)] = (
                            y[32 * rho:32 * rho + 32, :].astype(BF16))


def _qkv_call(x2d, w_groups, pos4, inv_row, jobs):
    t = x2d.shape[0]
    step_rows = QKV_SUB * TILE
    nt = t // step_rows
    x_specs = [pl.BlockSpec((step_rows, LANES), functools.partial(lambda i, c: (i, c), c=c)) for c in range(8)]
    out_shapes, out_specs = [], []
    for d in DILATIONS:
        for _ in range(3):
            out_shapes.append(jax.ShapeDtypeStruct((t // d, d * GROUP_COL), BF16))
            out_specs.append(pl.BlockSpec((step_rows // d, d * GROUP_COL), lambda i: (i, 0)))
    outs = pl.pallas_call(
        functools.partial(_qkv_kernel, n_casts=len(jobs), n_steps=nt, jobs=jobs),
        out_shape=tuple(out_shapes) + tuple(j.out_shape for j in jobs),
        grid=(nt,),
        in_specs=x_specs + [_resident(w.shape) for w in w_groups]
                 + [pl.BlockSpec((step_rows // 4, LANES), lambda i: (i, 0)), _resident(inv_row.shape)]
                 + [j.in_spec for j in jobs],
        out_specs=tuple(out_specs) + tuple(j.out_spec for j in jobs),
        compiler_params=pltpu.CompilerParams(dimension_semantics=("arbitrary",), vmem_limit_bytes=VMEM_LIMIT),
        name="qkv_rope",
    )(*([x2d] * 8), *w_groups, pos4, inv_row, *[j.src for j in jobs])
    return outs[:9], outs[9:]


def _mask16(cond):
    return jnp.where(cond, 1.0, 0.0).astype(BF16) > 0


def _attn_block(q, keys_of, vals_of, bias_t, state_get, state_put, emit):
    lo16 = _mask16(lax.broadcasted_iota(jnp.int32, (ATTN_BLK, LANES), 1) < HEAD_DIM)
    lo = lax.broadcasted_iota(jnp.int32, (1, LANES), 1) < HEAD_DIM
    eye = jnp.where(lax.broadcasted_iota(jnp.int32, (2 * ATTN_BLK, LANES), 0) % ATTN_BLK
                    == lax.broadcasted_iota(jnp.int32, (2 * ATTN_BLK, LANES), 1), 1.0, 0.0).astype(BF16)
    for j in range(HEADS // 2):
        sl = slice(LANES * j, LANES * (j + 1))
        qj = q[:, sl]
        zero = jnp.zeros_like(qj)
        lhs = jnp.concatenate([jnp.where(lo16, qj, zero), jnp.where(lo16, zero, qj)], axis=0)
        scores = lax.dot_general(jnp.concatenate([lhs, eye], axis=1),
                                 jnp.concatenate([keys_of(sl), bias_t], axis=1),
                                 (((1,), (1,)), ((), ())), preferred_element_type=F32)
        vals = vals_of(sl)
        rhs = jnp.concatenate([vals, jnp.ones_like(vals)], axis=1)
        outs, maxes = [], []
        for hl in range(2):
            s = scores[ATTN_BLK * hl:ATTN_BLK * (hl + 1), :]
            mb = jnp.max(s, axis=1, keepdims=True)
            p = jnp.exp2(s - mb).astype(BF16)
            outs.append(jnp.dot(p, rhs, preferred_element_type=F32))
            maxes.append(mb)
        o_new = jnp.where(lo, outs[0][:, :LANES], outs[1][:, :LANES])
        l_new = jnp.where(lo, outs[0][:, LANES:], outs[1][:, LANES:])
        m_new = jnp.where(lo, maxes[0], maxes[1])
        old = state_get(j)
        if old is not None:
            acc0, l0, m0 = old
            m = jnp.maximum(m0, m_new)
            a0 = jnp.exp2(m0 - m)
            a1 = jnp.exp2(m_new - m)
            o_new = acc0 * a0 + o_new * a1
            l_new = l0 * a0 + l_new * a1
            m_new = m
        if emit is not None:
            emit(j, o_new / l_new)
        else:
            state_put(j, o_new, l_new, m_new)


def _attn_kernel(*refs, n_casts, n_steps, jobs):
    q1, k1, v1, kp1, vp1, q2, k2, v2, kp2, vp2, q3, k3, v3, kp3, vp3, mask_ref = refs[:16]
    cast_src = refs[16:16 + n_casts]
    out_ref = refs[16 + n_casts]
    cast_dst = refs[17 + n_casts:17 + 2 * n_casts]
    acc_s, l_s, m_s = refs[17 + 2 * n_casts:]
    n = pl.program_id(1)
    s = pl.program_id(2)
    state = (acc_s, l_s, m_s)
    step_no = (pl.program_id(0) * pl.num_programs(1) + n) * pl.num_programs(2) + s
    _run_casts(jobs, cast_src, cast_dst, step_no, n_steps)

    def lanes(j):
        return slice(LANES * j, LANES * (j + 1))

    def bias_of(g, first):
        return mask_ref[g, 0] if first is None else mask_ref[g, first.astype(jnp.int32)]

    def two_blocks(prev_ref, cur_ref, col0):
        def load(sl):
            cols = slice(col0 + sl.start, col0 + sl.stop)
            return jnp.concatenate([prev_ref[:, cols], cur_ref[:ATTN_BLK, cols]], axis=0)
        return load

    spg = _STEPS_PER_GROUP

    @pl.when(s < spg)
    def _group1():
        for c in range(STEP_BLOCKS):
            k_blk = pl.multiple_of((STEP_BLOCKS * s + c) * SUBLANES, SUBLANES)

            def put(j, *vals, k_blk=k_blk):
                for ref, val in zip(state, vals):
                    for rho in range(N_RES):
                        ref[rho, pl.ds(k_blk, SUBLANES), lanes(j)] = val[SUBLANES * rho:SUBLANES * (rho + 1), :]

            rows = slice(ATTN_BLK * c, ATTN_BLK * (c + 1))
            if c == 0:
                keys_of = lambda sl: jnp.concatenate([kp1[:, sl], k1[:ATTN_BLK, sl]], axis=0)
                vals_of = lambda sl: jnp.concatenate([vp1[:, sl], v1[:ATTN_BLK, sl]], axis=0)
                first = jnp.logical_and(n == 0, s == 0)
            else:
                both = slice(ATTN_BLK * (c - 1), ATTN_BLK * (c + 1))
                keys_of = lambda sl, both=both: k1[both, sl]
                vals_of = lambda sl, both=both: v1[both, sl]
                first = None
            _attn_block(q1[rows, :], keys_of, vals_of, bias_of(0, first), lambda j: None, put, None)

    @pl.when(jnp.logical_and(s >= spg, s < 2 * spg))
    def _group2():
        t = s - spg
        for kk in range(STEP_BLOCKS // 4):
            base = pl.multiple_of((t * (STEP_BLOCKS // 4) + kk) * 32, 32)
            bias = bias_of(1, jnp.logical_and(n == 0, t == 0)) if kk == 0 else bias_of(1, None)
            rows = slice(ATTN_BLK * kk, ATTN_BLK * (kk + 1))
            both = slice(ATTN_BLK * (kk - 1), ATTN_BLK * (kk + 1))
            for a in range(4):
                def get(j, a=a, base=base):
                    return tuple(jnp.concatenate([ref[4 * b + a, pl.ds(base, 32), lanes(j)] for b in range(4)],
                                                 axis=0) for ref in state)

                def put(j, *vals, a=a, base=base):
                    for ref, val in zip(state, vals):
                        for b in range(4):
                            ref[4 * b + a, pl.ds(base, 32), lanes(j)] = val[32 * b:32 * (b + 1), :]

                col0 = GROUP_COL * a
                if kk == 0:
                    keys_of = two_blocks(kp2, k2, col0)
                    vals_of = two_blocks(vp2, v2, col0)
                else:
                    keys_of = lambda sl, col0=col0, both=both: k2[both, col0 + sl.start:col0 + sl.stop]
                    vals_of = lambda sl, col0=col0, both=both: v2[both, col0 + sl.start:col0 + sl.stop]
                _attn_block(q2[rows, col0:col0 + GROUP_COL], keys_of, vals_of, bias, get, put, None)

    @pl.when(s >= 2 * spg)
    def _group3():
        t = s - 2 * spg
        bias = bias_of(2, n == 0)
        for c in range(STEP_BLOCKS):
            rho = STEP_BLOCKS * t + c

            def get(j, rho=rho):
                return tuple(ref[rho, :, lanes(j)] for ref in state)

            def emit(j, y, c=c):
                out_ref[0, c, :, lanes(j)] = y.astype(BF16)

            col0 = GROUP_COL * c
            _attn_block(q3[:, col0:col0 + GROUP_COL], two_blocks(kp3, k3, col0), two_blocks(vp3, v3, col0),
                        bias, get, None, emit)


def _attn_call(qkv, masks, batch, seq, jobs):
    nchunk = seq // CHUNK
    spg = _STEPS_PER_GROUP
    rows1 = STEP_BLOCKS * ATTN_BLK
    rows2 = (STEP_BLOCKS // 4) * ATTN_BLK

    def step(s, g):
        return jnp.clip(s - spg * g, 0, spg - 1)

    def cur1(b, n, s):
        return (b * (seq // rows1) + spg * n + step(s, 0), 0)

    def prev1(b, n, s):
        return (jnp.maximum(STEP_BLOCKS * cur1(b, n, s)[0] - 1, 0), 0)

    def cur2(b, n, s):
        return (b * (seq // 4 // rows2) + spg * n + step(s, 1), 0)

    def prev2(b, n, s):
        return (jnp.maximum((STEP_BLOCKS // 4) * cur2(b, n, s)[0] - 1, 0), 0)

    def cur3(b, n, s):
        return (b * nchunk + n, step(s, 2))

    def prev3(b, n, s):
        return (jnp.maximum(b * nchunk + n - 1, 0), step(s, 2))

    blk1 = (rows1, GROUP_COL)
    blk2 = (rows2, 4 * GROUP_COL)
    prev2_blk = (ATTN_BLK, 4 * GROUP_COL)
    blk3 = (ATTN_BLK, STEP_BLOCKS * GROUP_COL)
    in_specs = [
        pl.BlockSpec(blk1, cur1), pl.BlockSpec(blk1, cur1), pl.BlockSpec(blk1, cur1),
        pl.BlockSpec((ATTN_BLK, GROUP_COL), prev1), pl.BlockSpec((ATTN_BLK, GROUP_COL), prev1),
        pl.BlockSpec(blk2, cur2), pl.BlockSpec(blk2, cur2), pl.BlockSpec(blk2, cur2),
        pl.BlockSpec(prev2_blk, prev2), pl.BlockSpec(prev2_blk, prev2),
        pl.BlockSpec(blk3, cur3), pl.BlockSpec(blk3, cur3), pl.BlockSpec(blk3, cur3),
        pl.BlockSpec(blk3, prev3), pl.BlockSpec(blk3, prev3),
        _resident(masks.shape),
    ]
    q1, k1, v1, q2, k2, v2, q3, k3, v3 = qkv
    grid = (batch, nchunk, N_GROUPS * spg)
    outs = pl.pallas_call(
        functools.partial(_attn_kernel, n_casts=len(jobs), n_steps=grid[0] * grid[1] * grid[2], jobs=jobs),
        out_shape=(jax.ShapeDtypeStruct((batch, N_RES, seq // N_RES, GROUP_COL), BF16),)
                  + tuple(j.out_shape for j in jobs),
        grid=grid,
        in_specs=in_specs + [j.in_spec for j in jobs],
        out_specs=(pl.BlockSpec((1, STEP_BLOCKS, ATTN_BLK, GROUP_COL), lambda b, n, s: (b, step(s, 2), n, 0)),)
                  + tuple(j.out_spec for j in jobs),
        scratch_shapes=[pltpu.VMEM((N_RES, ATTN_BLK, GROUP_COL), F32)] * 3,
        compiler_params=pltpu.CompilerParams(dimension_semantics=("arbitrary",) * 3, vmem_limit_bytes=VMEM_LIMIT),
        name="dilated_attention",
    )(q1, k1, v1, k1, v1, q2, k2, v2, k2, v2, q3, k3, v3, k3, v3, masks, *[j.src for j in jobs])
    return outs[0], outs[1:]


def _mix_kernel(*refs, tiles_per_batch, n_casts, n_steps, jobs):
    x_ref, y_ref, w_ref, vec_ref, ws_ref, bsb_ref, pa_ref, pb_ref, pc_ref, wo_ref = refs[:10]
    cast_src = refs[10:10 + n_casts]
    o_ref = refs[10 + n_casts]
    cast_dst = refs[11 + n_casts:11 + 2 * n_casts]
    zc_ref, ys_ref = refs[11 + 2 * n_casts:]
    i = pl.program_id(0)
    _run_casts(jobs, cast_src, cast_dst, i, n_steps)

    def vec(r):
        return vec_ref[r:r + 1, :]

    @pl.when(i % tiles_per_batch == 0)
    def _():
        zc_ref[...] = jnp.zeros_like(zc_ref)

    for rho in range(N_RES):
        piece = y_ref[0, rho].astype(F32)
        for sl in range(GROUP_COL // LANES):
            ys_ref[sl, pl.ds(rho, MIX_ROWS // N_RES, stride=N_RES), :] = piece[:, LANES * sl:LANES * (sl + 1)]

    tril = (lax.broadcasted_iota(jnp.int32, (GMLP_CHUNK, GMLP_CHUNK), 0)
            >= lax.broadcasted_iota(jnp.int32, (GMLP_CHUNK, GMLP_CHUNK), 1))
    w_spatial = [jnp.where(tril, ws_ref[g], 0.0).astype(BF16) for g in range(GMLP_GROUPS)]
    sub = MIX_SLAB
    n_chunks = sub // GMLP_CHUNK
    tail = zc_ref[...]

    for h in range(MIX_ROWS // MIX_SLAB):
        rows = slice(sub * h, sub * (h + 1))
        x = x_ref[rows, :]
        xb = x.astype(BF16)

        def proj(k, xb=xb):
            return jnp.dot(xb, w_ref[:, D_MODEL * k:D_MODEL * (k + 1)], preferred_element_type=F32)

        z = proj(4) * proj(5)
        head = jnp.concatenate([tail, z[:SUBLANES, :]], axis=0)
        tail = z[sub - SUBLANES:, :]
        conv = vec(_P_CONV + 2) * z
        for shift in (1, 2):
            zs = jnp.concatenate([pltpu.roll(head, shift, 0)[SUBLANES:, :], pltpu.roll(z, shift, 0)[SUBLANES:, :]],
                                 axis=0)
            conv = conv + vec(_P_CONV + 2 - shift) * zs
        ya = (proj(3) * conv).astype(BF16)
        m = jax.nn.sigmoid(proj(0)) * jnp.dot(ya, pa_ref[...], preferred_element_type=F32)

        u = _gelu(proj(6))
        vb = _layer_norm(_gelu(proj(7)), vec(_P_GLN_G), vec(_P_GLN_B)).astype(BF16)
        cols = []
        for g in range(GMLP_GROUPS):
            gl = slice(LANES * g, LANES * (g + 1))
            rhs = jnp.concatenate([vb[GMLP_CHUNK * c:GMLP_CHUNK * (c + 1), gl] for c in range(n_chunks)], axis=1)
            sp = jnp.dot(w_spatial[g], rhs, preferred_element_type=F32)
            bias = bsb_ref[:, gl]
            cols.append(jnp.concatenate([sp[:, LANES * c:LANES * (c + 1)] + bias for c in range(n_chunks)], axis=0))
        yc = (u * jnp.concatenate(cols, axis=1)).astype(BF16)
        m = m + jax.nn.sigmoid(proj(2)) * jnp.dot(yc, pc_ref[...], preferred_element_type=F32)

        yb = jnp.concatenate([ys_ref[sl, rows, :] for sl in range(GROUP_COL // LANES)], axis=1).astype(BF16)
        m = m + jax.nn.sigmoid(proj(1)) * jnp.dot(yb, pb_ref[...], preferred_element_type=F32)

        mixed = jnp.dot(m.astype(BF16), wo_ref[...], preferred_element_type=F32)
        o_ref[rows, :] = _layer_norm(ALPHA * x + mixed, vec(_P_LN1_G), vec(_P_LN1_B))

    zc_ref[...] = tail


def _mix_call(x2d, y_attn, w_rest, vecs, w_s, bsb, pa, pb, pc, wo, layer, seq, jobs):
    t = x2d.shape[0]
    nt = t // MIX_ROWS
    tiles_per_batch = seq // MIX_ROWS
    outs = pl.pallas_call(
        functools.partial(_mix_kernel, tiles_per_batch=tiles_per_batch, n_casts=len(jobs), n_steps=nt, jobs=jobs),
        out_shape=(jax.ShapeDtypeStruct((t, D_MODEL), F32),) + tuple(j.out_shape for j in jobs),
        grid=(nt,),
        in_specs=[pl.BlockSpec((MIX_ROWS, D_MODEL), lambda i: (i, 0)),
                  pl.BlockSpec((1, N_RES, MIX_ROWS // N_RES, GROUP_COL),
                               lambda i: (i // tiles_per_batch, 0, i % tiles_per_batch, 0)),
                  _resident(w_rest.shape), _layer_resident(vecs.shape, layer), _layer_resident(w_s.shape, layer),
                  _layer_resident(bsb.shape, layer)]
                 + [_resident(w.shape) for w in (pa, pb, pc, wo)] + [j.in_spec for j in jobs],
        out_specs=(pl.BlockSpec((MIX_ROWS, D_MODEL), lambda i: (i, 0)),) + tuple(j.out_spec for j in jobs),
        scratch_shapes=[pltpu.VMEM((SUBLANES, D_MODEL), F32),
                        pltpu.VMEM((GROUP_COL // LANES, MIX_ROWS, LANES), F32)],
        compiler_params=pltpu.CompilerParams(dimension_semantics=("arbitrary",), vmem_limit_bytes=VMEM_LIMIT),
        name="mixers_out_ln",
    )(x2d, y_attn, w_rest, vecs, w_s, bsb, pa, pb, pc, wo, *[j.src for j in jobs])
    return outs[0], outs[1:]


def _ffn_kernel(*refs, n_casts, n_steps, jobs):
    x_ref, wg_ref, wu_ref, wd_ref, vec_ref = refs[:5]
    cast_src = refs[5:5 + n_casts]
    o_ref = refs[5 + n_casts]
    cast_dst = refs[6 + n_casts:]
    _run_casts(jobs, cast_src, cast_dst, pl.program_id(0), n_steps)

    x = x_ref[...]
    xb = x.astype(BF16)
    gate = jnp.dot(xb, wg_ref[...], preferred_element_type=F32)
    up = jnp.dot(xb, wu_ref[...], preferred_element_type=F32)
    h = (jax.nn.silu(gate) * up).astype(BF16)
    down = jnp.dot(h, wd_ref[...], preferred_element_type=F32)
    o_ref[...] = _layer_norm(ALPHA * x + down, vec_ref[_P_LN2_G:_P_LN2_G + 1, :], vec_ref[_P_LN2_B:_P_LN2_B + 1, :])


def _ffn_call(x2d, wg, wu, wd, vecs, layer, jobs):
    t = x2d.shape[0]
    nt = t // FFN_ROWS
    outs = pl.pallas_call(
        functools.partial(_ffn_kernel, n_casts=len(jobs), n_steps=nt, jobs=jobs),
        out_shape=(jax.ShapeDtypeStruct((t, D_MODEL), F32),) + tuple(j.out_shape for j in jobs),
        grid=(nt,),
        in_specs=[pl.BlockSpec((FFN_ROWS, D_MODEL), lambda i: (i, 0))] + [_resident(w.shape) for w in (wg, wu, wd)]
                 + [_layer_resident(vecs.shape, layer)] + [j.in_spec for j in jobs],
        out_specs=(pl.BlockSpec((FFN_ROWS, D_MODEL), lambda i: (i, 0)),) + tuple(j.out_spec for j in jobs),
        compiler_params=pltpu.CompilerParams(dimension_semantics=("arbitrary",), vmem_limit_bytes=VMEM_LIMIT),
        name="swiglu_ln",
    )(x2d, wg, wu, wd, vecs, *[j.src for j in jobs])
    return outs[0], outs[1:]


def _band_masks():
    u = np.arange(ATTN_BLK)
    stored = (
        (u % N_RES) * SUBLANES + u // N_RES,
        (u % 4) * 32 + u // 4,
        u,
    )
    out = np.empty((N_GROUPS, 2, 2 * ATTN_BLK, ATTN_BLK), np.float32)
    for g, st in enumerate(stored):
        orig = np.empty(ATTN_BLK, np.int64)
        orig[st] = u
        ki = orig[:, None]
        qi = orig[None, :]
        out[g, 0, :ATTN_BLK] = np.where(ki >= qi, 0.0, NEG)
        out[g, 1, :ATTN_BLK] = NEG
        out[g, :, ATTN_BLK:] = np.where(ki <= qi, 0.0, NEG)
    return out


def kernel(x, positions, w_in, conv_w, gmlp_ln_g, gmlp_ln_b, w_s, b_s, p_a, p_b, p_c, w_o, ln1_g, ln1_b, w_gate, w_up,
           w_down, ln2_g, ln2_b):
    batch, seq, d = x.shape
    t = batch * seq
    nt = t // TILE
    depth = w_in.shape[0]
    n_cast = _CAST_BLOCKS
    assert d == D_MODEL and seq % CHUNK == 0 and depth == DEPTH
    assert min(t // (QKV_SUB * TILE), t // FFN_ROWS) >= n_cast

    pos = positions.reshape(nt, ROWS_PER_RES, N_RES).transpose(0, 2, 1).reshape(nt, 4, TILE // 4).astype(F32)
    pos4 = jnp.repeat(pos.transpose(0, 2, 1), HALF, axis=2).reshape(t // 4, LANES)
    inv_freq = ROPE_THETA ** (-jnp.arange(HALF, dtype=F32) / HALF)
    inv_row = jnp.tile(inv_freq, LANES // HALF).reshape(1, LANES)

    masks = jnp.asarray(_band_masks(), dtype=BF16)
    vecs = jnp.concatenate([conv_w] + [v[:, None, :] for v in (gmlp_ln_g, gmlp_ln_b, ln1_g, ln1_b, ln2_g, ln2_b)],
                           axis=1)
    bsb = jnp.repeat(jnp.swapaxes(b_s, 1, 2), GMLP_CHUNK, axis=2)

    rest_bw = _REST_COLS // n_cast
    gap = (_OFF_GMLP - _OFF_ATTN) // rest_bw

    def rest_col_block(i):
        return jnp.where(i < _OFF_ATTN // rest_bw, i, i + gap)

    def attn_step(b, n, s):
        return (b * (seq // CHUNK) + n) * (N_GROUPS * _STEPS_PER_GROUP) + s

    def qkv_jobs(layer, n_blocks):
        return [_CastJob(w_in, layer, D_MODEL, _QKV_COLS, n_blocks, col_block=_OFF_ATTN // _QKV_COLS + g)
                for g in range(N_GROUPS)]

    w_groups = _cast_call(qkv_jobs(0, SUBLANES), SUBLANES)
    h = x.reshape(t, D_MODEL)
    for l in range(depth):
        jobs = [_CastJob(w_in, l, D_MODEL, _REST_COLS, n_cast, by_cols=True, col_map=rest_col_block),
                _CastJob(p_a, l, D_MODEL, D_MODEL, n_cast), _CastJob(p_b, l, GROUP_COL, D_MODEL, n_cast),
                _CastJob(p_c, l, D_MODEL, D_MODEL, n_cast), _CastJob(w_o, l, D_MODEL, D_MODEL, n_cast)]
        qkv, (w_rest, pa, pb, pc, wo) = _qkv_call(h, w_groups, pos4, inv_row, jobs)
        jobs = [_CastJob(w, l, w.shape[1], w.shape[2], n_cast, step_of=attn_step) for w in (w_gate, w_up, w_down)]
        y_attn, (wg, wu, wd) = _attn_call(qkv, masks, batch, seq, jobs)
        h, _ = _mix_call(h, y_attn, w_rest, vecs, w_s, bsb, pa, pb, pc, wo, l, seq, [])
        jobs = qkv_jobs(l + 1, n_cast) if l + 1 < depth else []
        h, w_next = _ffn_call(h, wg, wu, wd, vecs, l, jobs)
        w_groups = tuple(w_next)
    return h.reshape(batch, seq, D_MODEL)
```

```python
import functools

import numpy as np
import jax
import jax.numpy as jnp
from jax import lax
from jax.experimental import pallas as pl
from jax.experimental.pallas import tpu as pltpu

F32 = jnp.float32
BF16 = jnp.bfloat16

D_MODEL = 1024
HEADS = 8
HEAD_DIM = 64
HALF = HEAD_DIM // 2
GROUP_COL = HEADS * HEAD_DIM
DILATIONS = (1, 4, 16)
N_GROUPS = len(DILATIONS)
ATTN_BLK = 128
N_RES = 16
CHUNK = N_RES * ATTN_BLK
STEP_BLOCKS = 8
_STEPS_PER_GROUP = N_RES // STEP_BLOCKS
GMLP_GROUPS = 8
GMLP_CHUNK = 128
D_FF = 2816
DEPTH = 2
ALPHA = (2 * DEPTH) ** 0.25
LN_EPS = 1e-5
ROPE_THETA = 10000.0
LOG2_E = 1.4426950408889634

TILE = 512
QKV_SUB = 2
FFN_ROWS = 1024
ROWS_PER_RES = TILE // N_RES
LANES = 128
SUBLANES = 8
NEG = -1e30
VMEM_LIMIT = 56 * 1024 * 1024

_OFF_ATTN = 6 * D_MODEL
_QKV_COLS = 3 * GROUP_COL
_OFF_GMLP = _OFF_ATTN + N_GROUPS * _QKV_COLS
_REST_COLS = _OFF_ATTN + 2 * D_MODEL
_CAST_BLOCKS = 16

_P_CONV, _P_GLN_G, _P_GLN_B, _P_LN1_G, _P_LN1_B, _P_LN2_G, _P_LN2_B = 0, 3, 4, 5, 6, 7, 8


def _resident(shape):
    nd = len(shape)
    return pl.BlockSpec(shape, lambda *_: (0,) * nd, pipeline_mode=pl.Buffered(1))


def _layer_resident(shape, layer):
    nd = len(shape)
    return pl.BlockSpec((None,) + tuple(shape[1:]), lambda *_: (layer,) + (0,) * (nd - 1),
                        pipeline_mode=pl.Buffered(1))


def _layer_norm(v, g, b):
    mu = jnp.mean(v, axis=-1, keepdims=True)
    c = v - mu
    var = jnp.mean(c * c, axis=-1, keepdims=True)
    return c * lax.rsqrt(var + LN_EPS) * g + b


def _gelu(v):
    return 0.5 * v * (1.0 + lax.erf(v * 0.7071067811865476))


class _CastJob:
    def __init__(self, src, layer, rows, cols, n_blocks, *, col_block=0, by_cols=False, col_map=None):
        self.src, self.n_blocks, self.cols = src, n_blocks, cols
        last = n_blocks - 1
        if by_cols:
            bw = cols // n_blocks
            self.in_spec = pl.BlockSpec((None, rows, bw), lambda i: (layer, 0, col_map(jnp.minimum(i, last))))
            self.out_spec = pl.BlockSpec((rows, bw), lambda i: (0, jnp.minimum(i, last)))
            self.out_shape = jax.ShapeDtypeStruct((rows, cols), BF16)
        else:
            rb = rows // n_blocks
            out_cols = cols + LANES if cols == D_MODEL else cols
            self.in_spec = pl.BlockSpec((None, rb, cols), lambda i: (layer, jnp.minimum(i, last), col_block))
            self.out_spec = pl.BlockSpec((rb, out_cols), lambda i: (jnp.minimum(i, last), 0))
            self.out_shape = jax.ShapeDtypeStruct((rows, out_cols), BF16)

    def run(self, s, d):
        if d.shape[1] == s.shape[1]:
            d[...] = s[...].astype(BF16)
        else:
            d[:, :self.cols] = s[...].astype(BF16)
            d[:, self.cols:] = jnp.zeros((d.shape[0], d.shape[1] - self.cols), BF16)


def _run_casts(jobs, src_refs, dst_refs, step, n_steps):
    for job, s, d in zip(jobs, src_refs, dst_refs):
        if job.n_blocks >= n_steps:
            job.run(s, d)
        else:
            pl.when(step < job.n_blocks)(functools.partial(job.run, s, d))


def _cast_kernel(*refs, jobs):
    n = len(jobs)
    for job, s, d in zip(jobs, refs[:n], refs[n:]):
        job.run(s, d)


def _cast_call(jobs, n_steps):
    return pl.pallas_call(
        functools.partial(_cast_kernel, jobs=jobs),
        out_shape=tuple(j.out_shape for j in jobs),
        grid=(n_steps,),
        in_specs=[j.in_spec for j in jobs],
        out_specs=tuple(j.out_spec for j in jobs),
        compiler_params=pltpu.CompilerParams(dimension_semantics=("arbitrary",)),
        name="cast_weights",
    )(*[j.src for j in jobs])


def _rope_tables(pos4, inv_row):
    ang = pos4 * inv_row
    lane = lax.broadcasted_iota(jnp.int32, (TILE // 4, LANES), 1)
    group = lane // HALF
    sign = jnp.where((lane % HEAD_DIM) < HALF, -1.0, 1.0)
    out = []
    for dense, sgn in ((jnp.cos(ang), None), (jnp.sin(ang), sign)):
        rolled = [dense] + [pltpu.roll(dense, HALF * k, 1) for k in range(1, 4)]
        slabs = []
        for a in range(4):
            e = rolled[(0 - a) % 4]
            for g in range(1, 4):
                e = jnp.where(group == g, rolled[(g - a) % 4], e)
            slabs.append(e if sgn is None else e * sgn)
        out.append(jnp.concatenate(slabs, axis=0))
    return out


def _qkv_kernel(*refs, n_casts, n_steps, jobs):
    x_refs = refs[:8]
    w_refs = refs[8:11]
    pos_ref, inv_ref = refs[11:13]
    cast_src = refs[13:13 + n_casts]
    out_refs = refs[13 + n_casts:22 + n_casts]
    cast_dst = refs[22 + n_casts:]
    _run_casts(jobs, cast_src, cast_dst, pl.program_id(0), n_steps)

    scale = HEAD_DIM ** -0.5 * LOG2_E
    first_half = (lax.broadcasted_iota(jnp.int32, (TILE, LANES), 1) % HEAD_DIM) < HALF

    for h in range(QKV_SUB):
        rows = [jnp.concatenate([xr[pl.ds(TILE * h + rho, ROWS_PER_RES, stride=N_RES), :] for xr in x_refs], axis=1)
                for rho in range(N_RES)]
        xp = jnp.concatenate(rows, axis=0).astype(BF16)
        cos, sin = _rope_tables(pos_ref[TILE // 4 * h:TILE // 4 * (h + 1), :], inv_ref[...])
        tables = ((cos * scale, sin * scale), (cos, sin))

        for g in range(N_GROUPS):
            for j in range(3):
                y = jnp.dot(xp, w_refs[g][:, GROUP_COL * j:GROUP_COL * (j + 1)], preferred_element_type=F32)
                if j < 2:
                    c, s = tables[j]
                    parts = []
                    for sl in range(GROUP_COL // LANES):
                        a = y[:, LANES * sl:LANES * (sl + 1)]
                        partner = jnp.where(first_half, pltpu.roll(a, LANES - HALF, 1), pltpu.roll(a, HALF, 1))
                        parts.append(a * c + partner * s)
                    y = jnp.concatenate(parts, axis=1)
                o_ref = out_refs[3 * g + j]
                if g == 0:
                    pieces = [y[32 * rho + 8 * c:32 * rho + 8 * c + 8, :] for c in range(4) for rho in range(N_RES)]
                    o_ref[TILE * h:TILE * (h + 1), :] = jnp.concatenate(pieces, axis=0).astype(BF16)
                elif g == 1:
                    for a in range(4):
                        slab = jnp.concatenate([y[32 * (4 * b + a):32 * (4 * b + a) + 32, :] for b in range(4)],
                                               axis=0)
                        o_ref[ATTN_BLK * h:ATTN_BLK * (h + 1), GROUP_COL * a:GROUP_COL * (a + 1)] = slab.astype(BF16)
                else:
                    for rho in range(N_RES):
                        o_ref[ROWS_PER_RES * h:ROWS_PER_RES * (h + 1), GROUP_COL * rho:GROUP_COL * (rho + 1)] = (
                            y[32 * rho:32 * rho + 32, :].astype(BF16))


def _qkv_call(x2d, w_groups, pos4, inv_row, jobs):
    t = x2d.shape[0]
    step_rows = QKV_SUB * TILE
    nt = t // step_rows
    x_specs = [pl.BlockSpec((step_rows, LANES), functools.partial(lambda i, c: (i, c), c=c)) for c in range(8)]
    out_shapes, out_specs = [], []
    for d in DILATIONS:
        for _ in range(3):
            out_shapes.append(jax.ShapeDtypeStruct((t // d, d * GROUP_COL), BF16))
            out_specs.append(pl.BlockSpec((step_rows // d, d * GROUP_COL), lambda i: (i, 0)))
    outs = pl.pallas_call(
        functools.partial(_qkv_kernel, n_casts=len(jobs), n_steps=nt, jobs=jobs),
        out_shape=tuple(out_shapes) + tuple(j.out_shape for j in jobs),
        grid=(nt,),
        in_specs=x_specs + [_resident(w.shape) for w in w_groups]
                 + [pl.BlockSpec((step_rows // 4, LANES), lambda i: (i, 0)), _resident(inv_row.shape)]
                 + [j.in_spec for j in jobs],
        out_specs=tuple(out_specs) + tuple(j.out_spec for j in jobs),
        compiler_params=pltpu.CompilerParams(dimension_semantics=("arbitrary",), vmem_limit_bytes=VMEM_LIMIT),
        name="qkv_rope",
    )(*([x2d] * 8), *w_groups, pos4, inv_row, *[j.src for j in jobs])
    return outs[:9], outs[9:]


def _mask16(cond):
    return jnp.where(cond, 1.0, 0.0).astype(BF16) > 0


def _attn_block(q, keys_of, vals_of, bias_t, state_get, state_put, emit):
    lo16 = _mask16(lax.broadcasted_iota(jnp.int32, (ATTN_BLK, LANES), 1) < HEAD_DIM)
    lo = lax.broadcasted_iota(jnp.int32, (1, LANES), 1) < HEAD_DIM
    eye = jnp.where(lax.broadcasted_iota(jnp.int32, (2 * ATTN_BLK, LANES), 0) % ATTN_BLK
                    == lax.broadcasted_iota(jnp.int32, (2 * ATTN_BLK, LANES), 1), 1.0, 0.0).astype(BF16)
    for j in range(HEADS // 2):
        sl = slice(LANES * j, LANES * (j + 1))
        qj = q[:, sl]
        zero = jnp.zeros_like(qj)
        lhs = jnp.concatenate([jnp.where(lo16, qj, zero), jnp.where(lo16, zero, qj)], axis=0)
        scores = lax.dot_general(jnp.concatenate([lhs, eye], axis=1),
                                 jnp.concatenate([keys_of(sl), bias_t], axis=1),
                                 (((1,), (1,)), ((), ())), preferred_element_type=F32)
        vals = vals_of(sl)
        rhs = jnp.concatenate([vals, jnp.ones_like(vals)], axis=1)
        outs, maxes = [], []
        for hl in range(2):
            s = scores[ATTN_BLK * hl:ATTN_BLK * (hl + 1), :]
            mb = jnp.max(s, axis=1, keepdims=True)
            p = jnp.exp2(s - mb).astype(BF16)
            outs.append(jnp.dot(p, rhs, preferred_element_type=F32))
            maxes.append(mb)
        o_new = jnp.where(lo, outs[0][:, :LANES], outs[1][:, :LANES])
        l_new = jnp.where(lo, outs[0][:, LANES:], outs[1][:, LANES:])
        m_new = jnp.where(lo, maxes[0], maxes[1])
        old = state_get(j)
        if old is not None:
            acc0, l0, m0 = old
            m = jnp.maximum(m0, m_new)
            a0 = jnp.exp2(m0 - m)
            a1 = jnp.exp2(m_new - m)
            o_new = acc0 * a0 + o_new * a1
            l_new = l0 * a0 + l_new * a1
            m_new = m
        if emit is not None:
            emit(j, o_new / l_new)
        else:
            state_put(j, o_new, l_new, m_new)


def _attn_kernel(q1, k1, v1, kp1, vp1, q2, k2, v2, kp2, vp2, q3, k3, v3, kp3, vp3, mask_ref, out_ref,
                 acc_s, l_s, m_s):
    n = pl.program_id(1)
    s = pl.program_id(2)
    state = (acc_s, l_s, m_s)

    def lanes(j):
        return slice(LANES * j, LANES * (j + 1))

    def bias_of(g, first):
        return mask_ref[g, 0] if first is None else mask_ref[g, first.astype(jnp.int32)]

    def two_blocks(prev_ref, cur_ref, col0):
        def load(sl):
            cols = slice(col0 + sl.start, col0 + sl.stop)
            return jnp.concatenate([prev_ref[:, cols], cur_ref[:ATTN_BLK, cols]], axis=0)
        return load

    spg = _STEPS_PER_GROUP

    @pl.when(s < spg)
    def _group1():
        for c in range(STEP_BLOCKS):
            k_blk = pl.multiple_of((STEP_BLOCKS * s + c) * SUBLANES, SUBLANES)

            def put(j, *vals, k_blk=k_blk):
                for ref, val in zip(state, vals):
                    for rho in range(N_RES):
                        ref[rho, pl.ds(k_blk, SUBLANES), lanes(j)] = val[SUBLANES * rho:SUBLANES * (rho + 1), :]

            rows = slice(ATTN_BLK * c, ATTN_BLK * (c + 1))
            if c == 0:
                keys_of = lambda sl: jnp.concatenate([kp1[:, sl], k1[:ATTN_BLK, sl]], axis=0)
                vals_of = lambda sl: jnp.concatenate([vp1[:, sl], v1[:ATTN_BLK, sl]], axis=0)
                first = jnp.logical_and(n == 0, s == 0)
            else:
                both = slice(ATTN_BLK * (c - 1), ATTN_BLK * (c + 1))
                keys_of = lambda sl, both=both: k1[both, sl]
                vals_of = lambda sl, both=both: v1[both, sl]
                first = None
            _attn_block(q1[rows, :], keys_of, vals_of, bias_of(0, first), lambda j: None, put, None)

    @pl.when(jnp.logical_and(s >= spg, s < 2 * spg))
    def _group2():
        t = s - spg
        for kk in range(STEP_BLOCKS // 4):
            base = pl.multiple_of((t * (STEP_BLOCKS // 4) + kk) * 32, 32)
            bias = bias_of(1, jnp.logical_and(n == 0, t == 0)) if kk == 0 else bias_of(1, None)
            rows = slice(ATTN_BLK * kk, ATTN_BLK * (kk + 1))
            both = slice(ATTN_BLK * (kk - 1), ATTN_BLK * (kk + 1))
            for a in range(4):
                def get(j, a=a, base=base):
                    return tuple(jnp.concatenate([ref[4 * b + a, pl.ds(base, 32), lanes(j)] for b in range(4)],
                                                 axis=0) for ref in state)

                def put(j, *vals, a=a, base=base):
                    for ref, val in zip(state, vals):
                        for b in range(4):
                            ref[4 * b + a, pl.ds(base, 32), lanes(j)] = val[32 * b:32 * (b + 1), :]

                col0 = GROUP_COL * a
                if kk == 0:
                    keys_of = two_blocks(kp2, k2, col0)
                    vals_of = two_blocks(vp2, v2, col0)
                else:
                    keys_of = lambda sl, col0=col0, both=both: k2[both, col0 + sl.start:col0 + sl.stop]
                    vals_of = lambda sl, col0=col0, both=both: v2[both, col0 + sl.start:col0 + sl.stop]
                _attn_block(q2[rows, col0:col0 + GROUP_COL], keys_of, vals_of, bias, get, put, None)

    @pl.when(s >= 2 * spg)
    def _group3():
        t = s - 2 * spg
        bias = bias_of(2, n == 0)
        for c in range(STEP_BLOCKS):
            rho = STEP_BLOCKS * t + c

            def get(j, rho=rho):
                return tuple(ref[rho, :, lanes(j)] for ref in state)

            def emit(j, y, c=c):
                out_ref[0, c, :, lanes(j)] = y.astype(BF16)

            col0 = GROUP_COL * c
            _attn_block(q3[:, col0:col0 + GROUP_COL], two_blocks(kp3, k3, col0), two_blocks(vp3, v3, col0),
                        bias, get, None, emit)


def _attn_call(qkv, masks, batch, seq):
    nchunk = seq // CHUNK
    spg = _STEPS_PER_GROUP
    rows1 = STEP_BLOCKS * ATTN_BLK
    rows2 = (STEP_BLOCKS // 4) * ATTN_BLK

    def step(s, g):
        return jnp.clip(s - spg * g, 0, spg - 1)

    def cur1(b, n, s):
        return (b * (seq // rows1) + spg * n + step(s, 0), 0)

    def prev1(b, n, s):
        return (jnp.maximum(STEP_BLOCKS * cur1(b, n, s)[0] - 1, 0), 0)

    def cur2(b, n, s):
        return (b * (seq // 4 // rows2) + spg * n + step(s, 1), 0)

    def prev2(b, n, s):
        return (jnp.maximum((STEP_BLOCKS // 4) * cur2(b, n, s)[0] - 1, 0), 0)

    def cur3(b, n, s):
        return (b * nchunk + n, step(s, 2))

    def prev3(b, n, s):
        return (jnp.maximum(b * nchunk + n - 1, 0), step(s, 2))

    blk1 = (rows1, GROUP_COL)
    blk2 = (rows2, 4 * GROUP_COL)
    prev2_blk = (ATTN_BLK, 4 * GROUP_COL)
    blk3 = (ATTN_BLK, STEP_BLOCKS * GROUP_COL)
    in_specs = [
        pl.BlockSpec(blk1, cur1), pl.BlockSpec(blk1, cur1), pl.BlockSpec(blk1, cur1),
        pl.BlockSpec((ATTN_BLK, GROUP_COL), prev1), pl.BlockSpec((ATTN_BLK, GROUP_COL), prev1),
        pl.BlockSpec(blk2, cur2), pl.BlockSpec(blk2, cur2), pl.BlockSpec(blk2, cur2),
        pl.BlockSpec(prev2_blk, prev2), pl.BlockSpec(prev2_blk, prev2),
        pl.BlockSpec(blk3, cur3), pl.BlockSpec(blk3, cur3), pl.BlockSpec(blk3, cur3),
        pl.BlockSpec(blk3, prev3), pl.BlockSpec(blk3, prev3),
        _resident(masks.shape),
    ]
    q1, k1, v1, q2, k2, v2, q3, k3, v3 = qkv
    return pl.pallas_call(
        _attn_kernel,
        out_shape=jax.ShapeDtypeStruct((batch, N_RES, seq // N_RES, GROUP_COL), BF16),
        grid=(batch, nchunk, N_GROUPS * spg),
        in_specs=in_specs,
        out_specs=pl.BlockSpec((1, STEP_BLOCKS, ATTN_BLK, GROUP_COL), lambda b, n, s: (b, step(s, 2), n, 0)),
        scratch_shapes=[pltpu.VMEM((N_RES, ATTN_BLK, GROUP_COL), F32)] * 3,
        compiler_params=pltpu.CompilerParams(dimension_semantics=("arbitrary",) * 3, vmem_limit_bytes=VMEM_LIMIT),
        name="dilated_attention",
    )(q1, k1, v1, k1, v1, q2, k2, v2, k2, v2, q3, k3, v3, k3, v3, masks)


def _mix_kernel(*refs, tiles_per_batch, n_casts, n_steps, jobs):
    x_ref, y_ref, w_ref, vec_ref, ws_ref, bsb_ref, pa_ref, pb_ref, pc_ref, wo_ref = refs[:10]
    cast_src = refs[10:10 + n_casts]
    o_ref = refs[10 + n_casts]
    cast_dst = refs[11 + n_casts:11 + 2 * n_casts]
    zc_ref, ys_ref = refs[11 + 2 * n_casts:]
    i = pl.program_id(0)
    _run_casts(jobs, cast_src, cast_dst, i, n_steps)

    x = x_ref[...]
    xb = x.astype(BF16)

    def vec(r):
        return vec_ref[r:r + 1, :]

    def proj(k):
        return jnp.dot(xb, w_ref[:, D_MODEL * k:D_MODEL * (k + 1)], preferred_element_type=F32)

    def out_proj(a, p_ref):
        return jnp.dot(a, p_ref[:, :D_MODEL], preferred_element_type=F32)

    @pl.when(i % tiles_per_batch == 0)
    def _():
        zc_ref[...] = jnp.zeros_like(zc_ref)

    z = proj(4) * proj(5)
    head = jnp.concatenate([zc_ref[...], z[:SUBLANES, :]], axis=0)
    zc_ref[...] = z[TILE - SUBLANES:, :]
    conv = vec(_P_CONV + 2) * z
    for shift in (1, 2):
        zs = jnp.concatenate([pltpu.roll(head, shift, 0)[SUBLANES:, :], pltpu.roll(z, shift, 0)[SUBLANES:, :]], axis=0)
        conv = conv + vec(_P_CONV + 2 - shift) * zs
    ya = (proj(3) * conv).astype(BF16)
    m = jax.nn.sigmoid(proj(0)) * out_proj(ya, pa_ref)

    u = _gelu(proj(6))
    vb = _layer_norm(_gelu(proj(7)), vec(_P_GLN_G), vec(_P_GLN_B)).astype(BF16)
    n_chunks = TILE // GMLP_CHUNK
    tril = (lax.broadcasted_iota(jnp.int32, (GMLP_CHUNK, GMLP_CHUNK), 0)
            >= lax.broadcasted_iota(jnp.int32, (GMLP_CHUNK, GMLP_CHUNK), 1))
    cols = []
    for g in range(GMLP_GROUPS):
        gl = slice(LANES * g, LANES * (g + 1))
        wg = jnp.where(tril, ws_ref[g], 0.0).astype(BF16)
        rhs = jnp.concatenate([vb[GMLP_CHUNK * c:GMLP_CHUNK * (c + 1), gl] for c in range(n_chunks)], axis=1)
        sp = jnp.dot(wg, rhs, preferred_element_type=F32)
        bias = bsb_ref[:, gl]
        cols.append(jnp.concatenate([sp[:, LANES * c:LANES * (c + 1)] + bias for c in range(n_chunks)], axis=0))
    yc = (u * jnp.concatenate(cols, axis=1)).astype(BF16)
    m = m + jax.nn.sigmoid(proj(2)) * out_proj(yc, pc_ref)

    for rho in range(N_RES):
        piece = y_ref[0, rho].astype(F32)
        for sl in range(GROUP_COL // LANES):
            ys_ref[sl, pl.ds(rho, ROWS_PER_RES, stride=N_RES), :] = piece[:, LANES * sl:LANES * (sl + 1)]
    yb = jnp.concatenate([ys_ref[sl] for sl in range(GROUP_COL // LANES)], axis=1).astype(BF16)
    m = m + jax.nn.sigmoid(proj(1)) * out_proj(yb, pb_ref)

    mixed = out_proj(m.astype(BF16), wo_ref)
    o_ref[...] = _layer_norm(ALPHA * x + mixed, vec(_P_LN1_G), vec(_P_LN1_B))


def _mix_call(x2d, y_attn, w_rest, vecs, w_s, bsb, pa, pb, pc, wo, layer, seq, jobs):
    t = x2d.shape[0]
    nt = t // TILE
    tiles_per_batch = seq // TILE
    outs = pl.pallas_call(
        functools.partial(_mix_kernel, tiles_per_batch=tiles_per_batch, n_casts=len(jobs), n_steps=nt, jobs=jobs),
        out_shape=(jax.ShapeDtypeStruct((t, D_MODEL), F32),) + tuple(j.out_shape for j in jobs),
        grid=(nt,),
        in_specs=[pl.BlockSpec((TILE, D_MODEL), lambda i: (i, 0)),
                  pl.BlockSpec((1, N_RES, ROWS_PER_RES, GROUP_COL),
                               lambda i: (i // tiles_per_batch, 0, i % tiles_per_batch, 0)),
                  _resident(w_rest.shape), _layer_resident(vecs.shape, layer), _layer_resident(w_s.shape, layer),
                  _layer_resident(bsb.shape, layer)]
                 + [_resident(w.shape) for w in (pa, pb, pc, wo)] + [j.in_spec for j in jobs],
        out_specs=(pl.BlockSpec((TILE, D_MODEL), lambda i: (i, 0)),) + tuple(j.out_spec for j in jobs),
        scratch_shapes=[pltpu.VMEM((SUBLANES, D_MODEL), F32),
                        pltpu.VMEM((GROUP_COL // LANES, TILE, LANES), F32)],
        compiler_params=pltpu.CompilerParams(dimension_semantics=("arbitrary",), vmem_limit_bytes=VMEM_LIMIT),
        name="mixers_out_ln",
    )(x2d, y_attn, w_rest, vecs, w_s, bsb, pa, pb, pc, wo, *[j.src for j in jobs])
    return outs[0], outs[1:]


def _ffn_kernel(*refs, n_casts, n_steps, jobs):
    x_ref, wg_ref, wu_ref, wd_ref, vec_ref = refs[:5]
    cast_src = refs[5:5 + n_casts]
    o_ref = refs[5 + n_casts]
    cast_dst = refs[6 + n_casts:]
    _run_casts(jobs, cast_src, cast_dst, pl.program_id(0), n_steps)

    x = x_ref[...]
    xb = x.astype(BF16)
    gate = jnp.dot(xb, wg_ref[...], preferred_element_type=F32)
    up = jnp.dot(xb, wu_ref[...], preferred_element_type=F32)
    h = (jax.nn.silu(gate) * up).astype(BF16)
    down = jnp.dot(h, wd_ref[:, :D_MODEL], preferred_element_type=F32)
    o_ref[...] = _layer_norm(ALPHA * x + down, vec_ref[_P_LN2_G:_P_LN2_G + 1, :], vec_ref[_P_LN2_B:_P_LN2_B + 1, :])


def _ffn_call(x2d, wg, wu, wd, vecs, layer, jobs):
    t = x2d.shape[0]
    nt = t // FFN_ROWS
    outs = pl.pallas_call(
        functools.partial(_ffn_kernel, n_casts=len(jobs), n_steps=nt, jobs=jobs),
        out_shape=(jax.ShapeDtypeStruct((t, D_MODEL), F32),) + tuple(j.out_shape for j in jobs),
        grid=(nt,),
        in_specs=[pl.BlockSpec((FFN_ROWS, D_MODEL), lambda i: (i, 0))] + [_resident(w.shape) for w in (wg, wu, wd)]
                 + [_layer_resident(vecs.shape, layer)] + [j.in_spec for j in jobs],
        out_specs=(pl.BlockSpec((FFN_ROWS, D_MODEL), lambda i: (i, 0)),) + tuple(j.out_spec for j in jobs),
        compiler_params=pltpu.CompilerParams(dimension_semantics=("arbitrary",), vmem_limit_bytes=VMEM_LIMIT),
        name="swiglu_ln",
    )(x2d, wg, wu, wd, vecs, *[j.src for j in jobs])
    return outs[0], outs[1:]


def _band_masks():
    u = np.arange(ATTN_BLK)
    stored = (
        (u % N_RES) * SUBLANES + u // N_RES,
        (u % 4) * 32 + u // 4,
        u,
    )
    out = np.empty((N_GROUPS, 2, 2 * ATTN_BLK, ATTN_BLK), np.float32)
    for g, st in enumerate(stored):
        orig = np.empty(ATTN_BLK, np.int64)
        orig[st] = u
        ki = orig[:, None]
        qi = orig[None, :]
        out[g, 0, :ATTN_BLK] = np.where(ki >= qi, 0.0, NEG)
        out[g, 1, :ATTN_BLK] = NEG
        out[g, :, ATTN_BLK:] = np.where(ki <= qi, 0.0, NEG)
    return out


def kernel(x, positions, w_in, conv_w, gmlp_ln_g, gmlp_ln_b, w_s, b_s, p_a, p_b, p_c, w_o, ln1_g, ln1_b, w_gate, w_up,
           w_down, ln2_g, ln2_b):
    batch, seq, d = x.shape
    t = batch * seq
    nt = t // TILE
    depth = w_in.shape[0]
    n_cast = _CAST_BLOCKS
    assert d == D_MODEL and seq % CHUNK == 0 and depth == DEPTH
    assert min(t // (QKV_SUB * TILE), t // FFN_ROWS, nt) >= n_cast

    pos = positions.reshape(nt, ROWS_PER_RES, N_RES).transpose(0, 2, 1).reshape(nt, 4, TILE // 4).astype(F32)
    pos4 = jnp.repeat(pos.transpose(0, 2, 1), HALF, axis=2).reshape(t // 4, LANES)
    inv_freq = ROPE_THETA ** (-jnp.arange(HALF, dtype=F32) / HALF)
    inv_row = jnp.tile(inv_freq, LANES // HALF).reshape(1, LANES)

    masks = jnp.asarray(_band_masks(), dtype=BF16)
    vecs = jnp.concatenate([conv_w] + [v[:, None, :] for v in (gmlp_ln_g, gmlp_ln_b, ln1_g, ln1_b, ln2_g, ln2_b)],
                           axis=1)
    bsb = jnp.repeat(jnp.swapaxes(b_s, 1, 2), GMLP_CHUNK, axis=2)

    rest_bw = _REST_COLS // n_cast
    gap = (_OFF_GMLP - _OFF_ATTN) // rest_bw

    def rest_col_block(i):
        return jnp.where(i < _OFF_ATTN // rest_bw, i, i + gap)

    def qkv_jobs(layer, n_blocks):
        return [_CastJob(w_in, layer, D_MODEL, _QKV_COLS, n_blocks, col_block=_OFF_ATTN // _QKV_COLS + g)
                for g in range(N_GROUPS)]

    w_groups = _cast_call(qkv_jobs(0, SUBLANES), SUBLANES)
    h = x.reshape(t, D_MODEL)
    for l in range(depth):
        jobs = [_CastJob(w_in, l, D_MODEL, _REST_COLS, n_cast, by_cols=True, col_map=rest_col_block),
                _CastJob(p_a, l, D_MODEL, D_MODEL, n_cast), _CastJob(p_b, l, GROUP_COL, D_MODEL, n_cast),
                _CastJob(p_c, l, D_MODEL, D_MODEL, n_cast), _CastJob(w_o, l, D_MODEL, D_MODEL, n_cast)]
        qkv, (w_rest, pa, pb, pc, wo) = _qkv_call(h, w_groups, pos4, inv_row, jobs)
        y_attn = _attn_call(qkv, masks, batch, seq)
        jobs = [_CastJob(w_gate, l, D_MODEL, D_FF, n_cast), _CastJob(w_up, l, D_MODEL, D_FF, n_cast),
                _CastJob(w_down, l, D_FF, D_MODEL, n_cast)]
        h, (wg, wu, wd) = _mix_call(h, y_attn, w_rest, vecs, w_s, bsb, pa, pb, pc, wo, l, seq, jobs)
        jobs = qkv_jobs(l + 1, n_cast) if l + 1 < depth else []
        h, w_next = _ffn_call(h, wg, wu, wd, vecs, l, jobs)
        w_groups = tuple(w_next)
    return h.reshape(batch, seq, D_MODEL)
```

```python
import functools

import numpy as np
import jax
import jax.numpy as jnp
from jax import lax
from jax.experimental import pallas as pl
from jax.experimental.pallas import tpu as pltpu

F32 = jnp.float32
BF16 = jnp.bfloat16

D_MODEL = 1024
HEADS = 8
HEAD_DIM = 64
HALF = HEAD_DIM // 2
GROUP_COL = HEADS * HEAD_DIM
DILATIONS = (1, 4, 16)
N_GROUPS = len(DILATIONS)
ATTN_BLK = 128
N_RES = 16
CHUNK = N_RES * ATTN_BLK
STEP_BLOCKS = 8
_STEPS_PER_GROUP = N_RES // STEP_BLOCKS
GMLP_GROUPS = 8
GMLP_CHUNK = 128
D_FF = 2816
DEPTH = 2
ALPHA = (2 * DEPTH) ** 0.25
LN_EPS = 1e-5
ROPE_THETA = 10000.0
LOG2_E = 1.4426950408889634

TILE = 512
QKV_SUB = 2
FFN_ROWS = 1024
ROWS_PER_RES = TILE // N_RES
LANES = 128
SUBLANES = 8
NEG = -1e30
VMEM_LIMIT = 56 * 1024 * 1024

_OFF_ATTN = 6 * D_MODEL
_QKV_COLS = 3 * GROUP_COL
_OFF_GMLP = _OFF_ATTN + N_GROUPS * _QKV_COLS
_REST_COLS = _OFF_ATTN + 2 * D_MODEL
_CAST_BLOCKS = 16

_P_CONV, _P_GLN_G, _P_GLN_B, _P_LN1_G, _P_LN1_B, _P_LN2_G, _P_LN2_B = 0, 3, 4, 5, 6, 7, 8


def _resident(shape):
    nd = len(shape)
    return pl.BlockSpec(shape, lambda *_: (0,) * nd, pipeline_mode=pl.Buffered(1))


def _layer_resident(shape, layer):
    nd = len(shape)
    return pl.BlockSpec((None,) + tuple(shape[1:]), lambda *_: (layer,) + (0,) * (nd - 1),
                        pipeline_mode=pl.Buffered(1))


def _layer_norm(v, g, b):
    mu = jnp.mean(v, axis=-1, keepdims=True)
    c = v - mu
    var = jnp.mean(c * c, axis=-1, keepdims=True)
    return c * lax.rsqrt(var + LN_EPS) * g + b


def _gelu(v):
    return 0.5 * v * (1.0 + lax.erf(v * 0.7071067811865476))


class _CastJob:
    def __init__(self, src, layer, rows, cols, n_blocks, *, col_block=0, by_cols=False, col_map=None):
        self.src, self.n_blocks, self.cols = src, n_blocks, cols
        last = n_blocks - 1
        if by_cols:
            bw = cols // n_blocks
            self.in_spec = pl.BlockSpec((None, rows, bw), lambda i: (layer, 0, col_map(jnp.minimum(i, last))))
            self.out_spec = pl.BlockSpec((rows, bw), lambda i: (0, jnp.minimum(i, last)))
            self.out_shape = jax.ShapeDtypeStruct((rows, cols), BF16)
        else:
            rb = rows // n_blocks
            out_cols = cols + LANES if cols == D_MODEL else cols
            self.in_spec = pl.BlockSpec((None, rb, cols), lambda i: (layer, jnp.minimum(i, last), col_block))
            self.out_spec = pl.BlockSpec((rb, out_cols), lambda i: (jnp.minimum(i, last), 0))
            self.out_shape = jax.ShapeDtypeStruct((rows, out_cols), BF16)

    def run(self, s, d):
        if d.shape[1] == s.shape[1]:
            d[...] = s[...].astype(BF16)
        else:
            d[:, :self.cols] = s[...].astype(BF16)
            d[:, self.cols:] = jnp.zeros((d.shape[0], d.shape[1] - self.cols), BF16)


def _run_casts(jobs, src_refs, dst_refs, step, n_steps):
    for job, s, d in zip(jobs, src_refs, dst_refs):
        if job.n_blocks >= n_steps:
            job.run(s, d)
        else:
            pl.when(step < job.n_blocks)(functools.partial(job.run, s, d))


def _cast_kernel(*refs, jobs):
    n = len(jobs)
    for job, s, d in zip(jobs, refs[:n], refs[n:]):
        job.run(s, d)


def _cast_call(jobs, n_steps):
    return pl.pallas_call(
        functools.partial(_cast_kernel, jobs=jobs),
        out_shape=tuple(j.out_shape for j in jobs),
        grid=(n_steps,),
        in_specs=[j.in_spec for j in jobs],
        out_specs=tuple(j.out_spec for j in jobs),
        compiler_params=pltpu.CompilerParams(dimension_semantics=("arbitrary",)),
        name="cast_weights",
    )(*[j.src for j in jobs])


def _rope_tables(pos4, inv_row):
    ang = pos4 * inv_row
    lane = lax.broadcasted_iota(jnp.int32, (TILE // 4, LANES), 1)
    group = lane // HALF
    sign = jnp.where((lane % HEAD_DIM) < HALF, -1.0, 1.0)
    out = []
    for dense, sgn in ((jnp.cos(ang), None), (jnp.sin(ang), sign)):
        rolled = [dense] + [pltpu.roll(dense, HALF * k, 1) for k in range(1, 4)]
        slabs = []
        for a in range(4):
            e = rolled[(0 - a) % 4]
            for g in range(1, 4):
                e = jnp.where(group == g, rolled[(g - a) % 4], e)
            slabs.append(e if sgn is None else e * sgn)
        out.append(jnp.concatenate(slabs, axis=0))
    return out


def _qkv_kernel(*refs, n_casts, n_steps, jobs):
    x_refs = refs[:8]
    w_refs = refs[8:11]
    pos_ref, inv_ref = refs[11:13]
    cast_src = refs[13:13 + n_casts]
    out_refs = refs[13 + n_casts:22 + n_casts]
    cast_dst = refs[22 + n_casts:]
    _run_casts(jobs, cast_src, cast_dst, pl.program_id(0), n_steps)

    scale = HEAD_DIM ** -0.5 * LOG2_E
    first_half = (lax.broadcasted_iota(jnp.int32, (TILE, LANES), 1) % HEAD_DIM) < HALF

    for h in range(QKV_SUB):
        rows = [jnp.concatenate([xr[pl.ds(TILE * h + rho, ROWS_PER_RES, stride=N_RES), :] for xr in x_refs], axis=1)
                for rho in range(N_RES)]
        xp = jnp.concatenate(rows, axis=0).astype(BF16)
        cos, sin = _rope_tables(pos_ref[TILE // 4 * h:TILE // 4 * (h + 1), :], inv_ref[...])
        tables = ((cos * scale, sin * scale), (cos, sin))

        for g in range(N_GROUPS):
            for j in range(3):
                y = jnp.dot(xp, w_refs[g][:, GROUP_COL * j:GROUP_COL * (j + 1)], preferred_element_type=F32)
                if j < 2:
                    c, s = tables[j]
                    parts = []
                    for sl in range(GROUP_COL // LANES):
                        a = y[:, LANES * sl:LANES * (sl + 1)]
                        partner = jnp.where(first_half, pltpu.roll(a, LANES - HALF, 1), pltpu.roll(a, HALF, 1))
                        parts.append(a * c + partner * s)
                    y = jnp.concatenate(parts, axis=1)
                o_ref = out_refs[3 * g + j]
                if g == 0:
                    pieces = [y[32 * rho + 8 * c:32 * rho + 8 * c + 8, :] for c in range(4) for rho in range(N_RES)]
                    o_ref[TILE * h:TILE * (h + 1), :] = jnp.concatenate(pieces, axis=0).astype(BF16)
                elif g == 1:
                    for a in range(4):
                        slab = jnp.concatenate([y[32 * (4 * b + a):32 * (4 * b + a) + 32, :] for b in range(4)],
                                               axis=0)
                        o_ref[ATTN_BLK * h:ATTN_BLK * (h + 1), GROUP_COL * a:GROUP_COL * (a + 1)] = slab.astype(BF16)
                else:
                    for rho in range(N_RES):
                        o_ref[ROWS_PER_RES * h:ROWS_PER_RES * (h + 1), GROUP_COL * rho:GROUP_COL * (rho + 1)] = (
                            y[32 * rho:32 * rho + 32, :].astype(BF16))


def _qkv_call(x2d, w_groups, pos4, inv_row, jobs):
    t = x2d.shape[0]
    step_rows = QKV_SUB * TILE
    nt = t // step_rows
    x_specs = [pl.BlockSpec((step_rows, LANES), functools.partial(lambda i, c: (i, c), c=c)) for c in range(8)]
    out_shapes, out_specs = [], []
    for d in DILATIONS:
        for _ in range(3):
            out_shapes.append(jax.ShapeDtypeStruct((t // d, d * GROUP_COL), BF16))
            out_specs.append(pl.BlockSpec((step_rows // d, d * GROUP_COL), lambda i: (i, 0)))
    outs = pl.pallas_call(
        functools.partial(_qkv_kernel, n_casts=len(jobs), n_steps=nt, jobs=jobs),
        out_shape=tuple(out_shapes) + tuple(j.out_shape for j in jobs),
        grid=(nt,),
        in_specs=x_specs + [_resident(w.shape) for w in w_groups]
                 + [pl.BlockSpec((step_rows // 4, LANES), lambda i: (i, 0)), _resident(inv_row.shape)]
                 + [j.in_spec for j in jobs],
        out_specs=tuple(out_specs) + tuple(j.out_spec for j in jobs),
        compiler_params=pltpu.CompilerParams(dimension_semantics=("arbitrary",), vmem_limit_bytes=VMEM_LIMIT),
        name="qkv_rope",
    )(*([x2d] * 8), *w_groups, pos4, inv_row, *[j.src for j in jobs])
    return outs[:9], outs[9:]


def _mask16(cond):
    return jnp.where(cond, 1.0, 0.0).astype(BF16) > 0


def _attn_block(q, keys_of, vals_of, bias_t, state_get, state_put, emit):
    lo16 = _mask16(lax.broadcasted_iota(jnp.int32, (ATTN_BLK, LANES), 1) < HEAD_DIM)
    lo = lax.broadcasted_iota(jnp.int32, (1, LANES), 1) < HEAD_DIM
    eye = jnp.where(lax.broadcasted_iota(jnp.int32, (2 * ATTN_BLK, LANES), 0) % ATTN_BLK
                    == lax.broadcasted_iota(jnp.int32, (2 * ATTN_BLK, LANES), 1), 1.0, 0.0).astype(BF16)
    for j in range(HEADS // 2):
        sl = slice(LANES * j, LANES * (j + 1))
        qj = q[:, sl]
        zero = jnp.zeros_like(qj)
        lhs = jnp.concatenate([jnp.where(lo16, qj, zero), jnp.where(lo16, zero, qj)], axis=0)
        scores = lax.dot_general(jnp.concatenate([lhs, eye], axis=1),
                                 jnp.concatenate([keys_of(sl), bias_t], axis=1),
                                 (((1,), (1,)), ((), ())), preferred_element_type=F32)
        vals = vals_of(sl)
        rhs = jnp.concatenate([vals, jnp.ones_like(vals)], axis=1)
        outs, maxes = [], []
        for hl in range(2):
            s = scores[ATTN_BLK * hl:ATTN_BLK * (hl + 1), :]
            mb = jnp.max(s, axis=1, keepdims=True)
            p = jnp.exp2(s - mb).astype(BF16)
            outs.append(jnp.dot(p, rhs, preferred_element_type=F32))
            maxes.append(mb)
        o_new = jnp.where(lo, outs[0][:, :LANES], outs[1][:, :LANES])
        l_new = jnp.where(lo, outs[0][:, LANES:], outs[1][:, LANES:])
        m_new = jnp.where(lo, maxes[0], maxes[1])
        old = state_get(j)
        if old is not None:
            acc0, l0, m0 = old
            m = jnp.maximum(m0, m_new)
            a0 = jnp.exp2(m0 - m)
            a1 = jnp.exp2(m_new - m)
            o_new = acc0 * a0 + o_new * a1
            l_new = l0 * a0 + l_new * a1
            m_new = m
        if emit is not None:
            emit(j, o_new / l_new)
        else:
            state_put(j, o_new, l_new, m_new)


def _attn_kernel(q1, k1, v1, kp1, vp1, q2, k2, v2, kp2, vp2, q3, k3, v3, kp3, vp3, mask_ref, out_ref,
                 acc_s, l_s, m_s):
    n = pl.program_id(1)
    s = pl.program_id(2)
    state = (acc_s, l_s, m_s)

    def lanes(j):
        return slice(LANES * j, LANES * (j + 1))

    def bias_of(g, first):
        return mask_ref[g, 0] if first is None else mask_ref[g, first.astype(jnp.int32)]

    def two_blocks(prev_ref, cur_ref, col0):
        def load(sl):
            cols = slice(col0 + sl.start, col0 + sl.stop)
            return jnp.concatenate([prev_ref[:, cols], cur_ref[:ATTN_BLK, cols]], axis=0)
        return load

    spg = _STEPS_PER_GROUP

    @pl.when(s < spg)
    def _group1():
        for c in range(STEP_BLOCKS):
            k_blk = pl.multiple_of((STEP_BLOCKS * s + c) * SUBLANES, SUBLANES)

            def put(j, *vals, k_blk=k_blk):
                for ref, val in zip(state, vals):
                    for rho in range(N_RES):
                        ref[rho, pl.ds(k_blk, SUBLANES), lanes(j)] = val[SUBLANES * rho:SUBLANES * (rho + 1), :]

            rows = slice(ATTN_BLK * c, ATTN_BLK * (c + 1))
            if c == 0:
                keys_of = lambda sl: jnp.concatenate([kp1[:, sl], k1[:ATTN_BLK, sl]], axis=0)
                vals_of = lambda sl: jnp.concatenate([vp1[:, sl], v1[:ATTN_BLK, sl]], axis=0)
                first = jnp.logical_and(n == 0, s == 0)
            else:
                both = slice(ATTN_BLK * (c - 1), ATTN_BLK * (c + 1))
                keys_of = lambda sl, both=both: k1[both, sl]
                vals_of = lambda sl, both=both: v1[both, sl]
                first = None
            _attn_block(q1[rows, :], keys_of, vals_of, bias_of(0, first), lambda j: None, put, None)

    @pl.when(jnp.logical_and(s >= spg, s < 2 * spg))
    def _group2():
        t = s - spg
        for kk in range(STEP_BLOCKS // 4):
            base = pl.multiple_of((t * (STEP_BLOCKS // 4) + kk) * 32, 32)
            bias = bias_of(1, jnp.logical_and(n == 0, t == 0)) if kk == 0 else bias_of(1, None)
            rows = slice(ATTN_BLK * kk, ATTN_BLK * (kk + 1))
            both = slice(ATTN_BLK * (kk - 1), ATTN_BLK * (kk + 1))
            for a in range(4):
                def get(j, a=a, base=base):
                    return tuple(jnp.concatenate([ref[4 * b + a, pl.ds(base, 32), lanes(j)] for b in range(4)],
                                                 axis=0) for ref in state)

                def put(j, *vals, a=a, base=base):
                    for ref, val in zip(state, vals):
                        for b in range(4):
                            ref[4 * b + a, pl.ds(base, 32), lanes(j)] = val[32 * b:32 * (b + 1), :]

                col0 = GROUP_COL * a
                if kk == 0:
                    keys_of = two_blocks(kp2, k2, col0)
                    vals_of = two_blocks(vp2, v2, col0)
                else:
                    keys_of = lambda sl, col0=col0, both=both: k2[both, col0 + sl.start:col0 + sl.stop]
                    vals_of = lambda sl, col0=col0, both=both: v2[both, col0 + sl.start:col0 + sl.stop]
                _attn_block(q2[rows, col0:col0 + GROUP_COL], keys_of, vals_of, bias, get, put, None)

    @pl.when(s >= 2 * spg)
    def _group3():
        t = s - 2 * spg
        bias = bias_of(2, n == 0)
        for c in range(STEP_BLOCKS):
            rho = STEP_BLOCKS * t + c

            def get(j, rho=rho):
                return tuple(ref[rho, :, lanes(j)] for ref in state)

            def emit(j, y, c=c):
                out_ref[0, c, :, lanes(j)] = y.astype(BF16)

            col0 = GROUP_COL * c
            _attn_block(q3[:, col0:col0 + GROUP_COL], two_blocks(kp3, k3, col0), two_blocks(vp3, v3, col0),
                        bias, get, None, emit)


def _attn_call(qkv, masks, batch, seq):
    nchunk = seq // CHUNK
    spg = _STEPS_PER_GROUP
    rows1 = STEP_BLOCKS * ATTN_BLK
    rows2 = (STEP_BLOCKS // 4) * ATTN_BLK

    def step(s, g):
        return jnp.clip(s - spg * g, 0, spg - 1)

    def cur1(b, n, s):
        return (b * (seq // rows1) + spg * n + step(s, 0), 0)

    def prev1(b, n, s):
        return (jnp.maximum(STEP_BLOCKS * cur1(b, n, s)[0] - 1, 0), 0)

    def cur2(b, n, s):
        return (b * (seq // 4 // rows2) + spg * n + step(s, 1), 0)

    def prev2(b, n, s):
        return (jnp.maximum((STEP_BLOCKS // 4) * cur2(b, n, s)[0] - 1, 0), 0)

    def cur3(b, n, s):
        return (b * nchunk + n, step(s, 2))

    def prev3(b, n, s):
        return (jnp.maximum(b * nchunk + n - 1, 0), step(s, 2))

    blk1 = (rows1, GROUP_COL)
    blk2 = (rows2, 4 * GROUP_COL)
    prev2_blk = (ATTN_BLK, 4 * GROUP_COL)
    blk3 = (ATTN_BLK, STEP_BLOCKS * GROUP_COL)
    in_specs = [
        pl.BlockSpec(blk1, cur1), pl.BlockSpec(blk1, cur1), pl.BlockSpec(blk1, cur1),
        pl.BlockSpec((ATTN_BLK, GROUP_COL), prev1), pl.BlockSpec((ATTN_BLK, GROUP_COL), prev1),
        pl.BlockSpec(blk2, cur2), pl.BlockSpec(blk2, cur2), pl.BlockSpec(blk2, cur2),
        pl.BlockSpec(prev2_blk, prev2), pl.BlockSpec(prev2_blk, prev2),
        pl.BlockSpec(blk3, cur3), pl.BlockSpec(blk3, cur3), pl.BlockSpec(blk3, cur3),
        pl.BlockSpec(blk3, prev3), pl.BlockSpec(blk3, prev3),
        _resident(masks.shape),
    ]
    q1, k1, v1, q2, k2, v2, q3, k3, v3 = qkv
    return pl.pallas_call(
        _attn_kernel,
        out_shape=jax.ShapeDtypeStruct((batch, N_RES, seq // N_RES, GROUP_COL), BF16),
        grid=(batch, nchunk, N_GROUPS * spg),
        in_specs=in_specs,
        out_specs=pl.BlockSpec((1, STEP_BLOCKS, ATTN_BLK, GROUP_COL), lambda b, n, s: (b, step(s, 2), n, 0)),
        scratch_shapes=[pltpu.VMEM((N_RES, ATTN_BLK, GROUP_COL), F32)] * 3,
        compiler_params=pltpu.CompilerParams(dimension_semantics=("arbitrary",) * 3, vmem_limit_bytes=VMEM_LIMIT),
        name="dilated_attention",
    )(q1, k1, v1, k1, v1, q2, k2, v2, k2, v2, q3, k3, v3, k3, v3, masks)


def _mix_kernel(*refs, tiles_per_batch, n_casts, n_steps, jobs):
    x_ref, y_ref, w_ref, vec_ref, ws_ref, bsb_ref, pa_ref, pb_ref, pc_ref, wo_ref = refs[:10]
    cast_src = refs[10:10 + n_casts]
    o_ref = refs[10 + n_casts]
    cast_dst = refs[11 + n_casts:11 + 2 * n_casts]
    zc_ref, ys_ref = refs[11 + 2 * n_casts:]
    i = pl.program_id(0)

    @pl.when(i % tiles_per_batch == 0)
    def _():
        zc_ref[...] = jnp.zeros_like(zc_ref)

    _run_casts(jobs, cast_src, cast_dst, i, n_steps)

    x = x_ref[...]
    xb = x.astype(BF16)

    def vec(r):
        return vec_ref[r:r + 1, :]

    def proj(k):
        return jnp.dot(xb, w_ref[:, D_MODEL * k:D_MODEL * (k + 1)], preferred_element_type=F32)

    def out_proj(a, p_ref):
        return jnp.dot(a, p_ref[:, :D_MODEL], preferred_element_type=F32)

    z = proj(4) * proj(5)
    head = jnp.concatenate([zc_ref[...], z[:SUBLANES, :]], axis=0)
    zc_ref[...] = z[TILE - SUBLANES:, :]
    conv = vec(_P_CONV + 2) * z
    for shift in (1, 2):
        zs = jnp.concatenate([pltpu.roll(head, shift, 0)[SUBLANES:, :], pltpu.roll(z, shift, 0)[SUBLANES:, :]], axis=0)
        conv = conv + vec(_P_CONV + 2 - shift) * zs
    ya = (proj(3) * conv).astype(BF16)
    m = jax.nn.sigmoid(proj(0)) * out_proj(ya, pa_ref)

    u = _gelu(proj(6))
    vb = _layer_norm(_gelu(proj(7)), vec(_P_GLN_G), vec(_P_GLN_B)).astype(BF16)
    n_chunks = TILE // GMLP_CHUNK
    tril = (lax.broadcasted_iota(jnp.int32, (GMLP_CHUNK, GMLP_CHUNK), 0)
            >= lax.broadcasted_iota(jnp.int32, (GMLP_CHUNK, GMLP_CHUNK), 1))
    cols = []
    for g in range(GMLP_GROUPS):
        gl = slice(LANES * g, LANES * (g + 1))
        wg = jnp.where(tril, ws_ref[g], 0.0).astype(BF16)
        rhs = jnp.concatenate([vb[GMLP_CHUNK * c:GMLP_CHUNK * (c + 1), gl] for c in range(n_chunks)], axis=1)
        sp = jnp.dot(wg, rhs, preferred_element_type=F32)
        bias = bsb_ref[:, gl]
        cols.append(jnp.concatenate([sp[:, LANES * c:LANES * (c + 1)] + bias for c in range(n_chunks)], axis=0))
    yc = (u * jnp.concatenate(cols, axis=1)).astype(BF16)
    m = m + jax.nn.sigmoid(proj(2)) * out_proj(yc, pc_ref)

    for rho in range(N_RES):
        piece = y_ref[0, rho].astype(F32)
        for sl in range(GROUP_COL // LANES):
            ys_ref[sl, pl.ds(rho, ROWS_PER_RES, stride=N_RES), :] = piece[:, LANES * sl:LANES * (sl + 1)]
    yb = jnp.concatenate([ys_ref[sl] for sl in range(GROUP_COL // LANES)], axis=1).astype(BF16)
    m = m + jax.nn.sigmoid(proj(1)) * out_proj(yb, pb_ref)

    mixed = out_proj(m.astype(BF16), wo_ref)
    o_ref[...] = _layer_norm(ALPHA * x + mixed, vec(_P_LN1_G), vec(_P_LN1_B))


def _mix_call(x2d, y_attn, w_rest, vecs, w_s, bsb, pa, pb, pc, wo, layer, seq, jobs):
    t = x2d.shape[0]
    nt = t // TILE
    tiles_per_batch = seq // TILE
    outs = pl.pallas_call(
        functools.partial(_mix_kernel, tiles_per_batch=tiles_per_batch, n_casts=len(jobs), n_steps=nt, jobs=jobs),
        out_shape=(jax.ShapeDtypeStruct((t, D_MODEL), F32),) + tuple(j.out_shape for j in jobs),
        grid=(nt,),
        in_specs=[pl.BlockSpec((TILE, D_MODEL), lambda i: (i, 0)),
                  pl.BlockSpec((1, N_RES, ROWS_PER_RES, GROUP_COL),
                               lambda i: (i // tiles_per_batch, 0, i % tiles_per_batch, 0)),
                  _resident(w_rest.shape), _layer_resident(vecs.shape, layer), _layer_resident(w_s.shape, layer),
                  _layer_resident(bsb.shape, layer)]
                 + [_resident(w.shape) for w in (pa, pb, pc, wo)] + [j.in_spec for j in jobs],
        out_specs=(pl.BlockSpec((TILE, D_MODEL), lambda i: (i, 0)),) + tuple(j.out_spec for j in jobs),
        scratch_shapes=[pltpu.VMEM((SUBLANES, D_MODEL), F32),
                        pltpu.VMEM((GROUP_COL // LANES, TILE, LANES), F32)],
        compiler_params=pltpu.CompilerParams(dimension_semantics=("arbitrary",), vmem_limit_bytes=VMEM_LIMIT),
        name="mixers_out_ln",
    )(x2d, y_attn, w_rest, vecs, w_s, bsb, pa, pb, pc, wo, *[j.src for j in jobs])
    return outs[0], outs[1:]


def _ffn_kernel(*refs, n_casts, n_steps, jobs):
    x_ref, wg_ref, wu_ref, wd_ref, vec_ref = refs[:5]
    cast_src = refs[5:5 + n_casts]
    o_ref = refs[5 + n_casts]
    cast_dst = refs[6 + n_casts:]
    _run_casts(jobs, cast_src, cast_dst, pl.program_id(0), n_steps)

    x = x_ref[...]
    xb = x.astype(BF16)
    gate = jnp.dot(xb, wg_ref[...], preferred_element_type=F32)
    up = jnp.dot(xb, wu_ref[...], preferred_element_type=F32)
    h = (jax.nn.silu(gate) * up).astype(BF16)
    down = jnp.dot(h, wd_ref[:, :D_MODEL], preferred_element_type=F32)
    o_ref[...] = _layer_norm(ALPHA * x + down, vec_ref[_P_LN2_G:_P_LN2_G + 1, :], vec_ref[_P_LN2_B:_P_LN2_B + 1, :])


def _ffn_call(x2d, wg, wu, wd, vecs, layer, jobs):
    t = x2d.shape[0]
    nt = t // FFN_ROWS
    outs = pl.pallas_call(
        functools.partial(_ffn_kernel, n_casts=len(jobs), n_steps=nt, jobs=jobs),
        out_shape=(jax.ShapeDtypeStruct((t, D_MODEL), F32),) + tuple(j.out_shape for j in jobs),
        grid=(nt,),
        in_specs=[pl.BlockSpec((FFN_ROWS, D_MODEL), lambda i: (i, 0))] + [_resident(w.shape) for w in (wg, wu, wd)]
                 + [_layer_resident(vecs.shape, layer)] + [j.in_spec for j in jobs],
        out_specs=(pl.BlockSpec((FFN_ROWS, D_MODEL), lambda i: (i, 0)),) + tuple(j.out_spec for j in jobs),
        compiler_params=pltpu.CompilerParams(dimension_semantics=("arbitrary",), vmem_limit_bytes=VMEM_LIMIT),
        name="swiglu_ln",
    )(x2d, wg, wu, wd, vecs, *[j.src for j in jobs])
    return outs[0], outs[1:]


def _band_masks():
    u = np.arange(ATTN_BLK)
    stored = (
        (u % N_RES) * SUBLANES + u // N_RES,
        (u % 4) * 32 + u // 4,
        u,
    )
    out = np.empty((N_GROUPS, 2, 2 * ATTN_BLK, ATTN_BLK), np.float32)
    for g, st in enumerate(stored):
        orig = np.empty(ATTN_BLK, np.int64)
        orig[st] = u
        ki = orig[:, None]
        qi = orig[None, :]
        out[g, 0, :ATTN_BLK] = np.where(ki >= qi, 0.0, NEG)
        out[g, 1, :ATTN_BLK] = NEG
        out[g, :, ATTN_BLK:] = np.where(ki <= qi, 0.0, NEG)
    return out


def kernel(x, positions, w_in, conv_w, gmlp_ln_g, gmlp_ln_b, w_s, b_s, p_a, p_b, p_c, w_o, ln1_g, ln1_b, w_gate, w_up,
           w_down, ln2_g, ln2_b):
    batch, seq, d = x.shape
    t = batch * seq
    nt = t // TILE
    depth = w_in.shape[0]
    n_cast = _CAST_BLOCKS
    assert d == D_MODEL and seq % CHUNK == 0 and depth == DEPTH
    assert min(t // (QKV_SUB * TILE), t // FFN_ROWS, nt) >= n_cast

    pos = positions.reshape(nt, ROWS_PER_RES, N_RES).transpose(0, 2, 1).reshape(nt, 4, TILE // 4).astype(F32)
    pos4 = jnp.repeat(pos.transpose(0, 2, 1), HALF, axis=2).reshape(t // 4, LANES)
    inv_freq = ROPE_THETA ** (-jnp.arange(HALF, dtype=F32) / HALF)
    inv_row = jnp.tile(inv_freq, LANES // HALF).reshape(1, LANES)

    masks = jnp.asarray(_band_masks(), dtype=BF16)
    vecs = jnp.concatenate([conv_w] + [v[:, None, :] for v in (gmlp_ln_g, gmlp_ln_b, ln1_g, ln1_b, ln2_g, ln2_b)],
                           axis=1)
    bsb = jnp.repeat(jnp.swapaxes(b_s, 1, 2), GMLP_CHUNK, axis=2)

    rest_bw = _REST_COLS // n_cast
    gap = (_OFF_GMLP - _OFF_ATTN) // rest_bw

    def rest_col_block(i):
        return jnp.where(i < _OFF_ATTN // rest_bw, i, i + gap)

    def qkv_jobs(layer, n_blocks):
        return [_CastJob(w_in, layer, D_MODEL, _QKV_COLS, n_blocks, col_block=_OFF_ATTN // _QKV_COLS + g)
                for g in range(N_GROUPS)]

    w_groups = _cast_call(qkv_jobs(0, SUBLANES), SUBLANES)
    h = x.reshape(t, D_MODEL)
    for l in range(depth):
        jobs = [_CastJob(w_in, l, D_MODEL, _REST_COLS, n_cast, by_cols=True, col_map=rest_col_block),
                _CastJob(p_a, l, D_MODEL, D_MODEL, n_cast), _CastJob(p_b, l, GROUP_COL, D_MODEL, n_cast),
                _CastJob(p_c, l, D_MODEL, D_MODEL, n_cast), _CastJob(w_o, l, D_MODEL, D_MODEL, n_cast),
                _CastJob(w_down, l, D_FF, D_MODEL, n_cast)]
        qkv, (w_rest, pa, pb, pc, wo, wd) = _qkv_call(h, w_groups, pos4, inv_row, jobs)
        y_attn = _attn_call(qkv, masks, batch, seq)
        jobs = [_CastJob(w_gate, l, D_MODEL, D_FF, nt), _CastJob(w_up, l, D_MODEL, D_FF, nt)]
        h, (wg, wu) = _mix_call(h, y_attn, w_rest, vecs, w_s, bsb, pa, pb, pc, wo, l, seq, jobs)
        jobs = qkv_jobs(l + 1, n_cast) if l + 1 < depth else []
        h, w_next = _ffn_call(h, wg, wu, wd, vecs, l, jobs)
        w_groups = tuple(w_next)
    return h.reshape(batch, seq, D_MODEL)
```

```python
import functools

import numpy as np
import jax
import jax.numpy as jnp
from jax import lax
from jax.experimental import pallas as pl
from jax.experimental.pallas import tpu as pltpu

F32 = jnp.float32
BF16 = jnp.bfloat16

D_MODEL = 1024
HEADS = 8
HEAD_DIM = 64
HALF = HEAD_DIM // 2
GROUP_COL = HEADS * HEAD_DIM
DILATIONS = (1, 4, 16)
N_GROUPS = len(DILATIONS)
ATTN_BLK = 128
N_RES = 16
CHUNK = N_RES * ATTN_BLK
STEP_BLOCKS = 8
_STEPS_PER_GROUP = N_RES // STEP_BLOCKS
GMLP_GROUPS = 8
GMLP_CHUNK = 128
D_FF = 2816
DEPTH = 2
ALPHA = (2 * DEPTH) ** 0.25
LN_EPS = 1e-5
ROPE_THETA = 10000.0
LOG2_E = 1.4426950408889634

TILE = 512
QKV_SUB = 2
FFN_ROWS = 1024
ROWS_PER_RES = TILE // N_RES
LANES = 128
SUBLANES = 8
NEG = -1e30
VMEM_LIMIT = 56 * 1024 * 1024

_OFF_ATTN = 6 * D_MODEL
_QKV_COLS = 3 * GROUP_COL
_OFF_GMLP = _OFF_ATTN + N_GROUPS * _QKV_COLS
_REST_COLS = _OFF_ATTN + 2 * D_MODEL
_CAST_BLOCKS = 16

_P_CONV, _P_GLN_G, _P_GLN_B, _P_LN1_G, _P_LN1_B, _P_LN2_G, _P_LN2_B = 0, 3, 4, 5, 6, 7, 8


def _resident(shape):
    nd = len(shape)
    return pl.BlockSpec(shape, lambda *_: (0,) * nd, pipeline_mode=pl.Buffered(1))


def _layer_resident(shape, layer):
    nd = len(shape)
    return pl.BlockSpec((None,) + tuple(shape[1:]), lambda *_: (layer,) + (0,) * (nd - 1),
                        pipeline_mode=pl.Buffered(1))


def _layer_norm(v, g, b):
    mu = jnp.mean(v, axis=-1, keepdims=True)
    c = v - mu
    var = jnp.mean(c * c, axis=-1, keepdims=True)
    return c * lax.rsqrt(var + LN_EPS) * g + b


def _gelu(v):
    return 0.5 * v * (1.0 + lax.erf(v * 0.7071067811865476))


class _CastJob:
    def __init__(self, src, layer, rows, cols, n_blocks, *, col_block=0, by_cols=False, col_map=None):
        self.src, self.n_blocks, self.cols = src, n_blocks, cols
        last = n_blocks - 1
        if by_cols:
            bw = cols // n_blocks
            self.in_spec = pl.BlockSpec((None, rows, bw), lambda i: (layer, 0, col_map(jnp.minimum(i, last))))
            self.out_spec = pl.BlockSpec((rows, bw), lambda i: (0, jnp.minimum(i, last)))
            self.out_shape = jax.ShapeDtypeStruct((rows, cols), BF16)
        else:
            rb = rows // n_blocks
            out_cols = cols + LANES if cols == D_MODEL else cols
            self.in_spec = pl.BlockSpec((None, rb, cols), lambda i: (layer, jnp.minimum(i, last), col_block))
            self.out_spec = pl.BlockSpec((rb, out_cols), lambda i: (jnp.minimum(i, last), 0))
            self.out_shape = jax.ShapeDtypeStruct((rows, out_cols), BF16)

    def run(self, s, d):
        if d.shape[1] == s.shape[1]:
            d[...] = s[...].astype(BF16)
        else:
            d[:, :self.cols] = s[...].astype(BF16)
            d[:, self.cols:] = jnp.zeros((d.shape[0], d.shape[1] - self.cols), BF16)


def _run_casts(jobs, src_refs, dst_refs, step, n_steps):
    for job, s, d in zip(jobs, src_refs, dst_refs):
        if job.n_blocks >= n_steps:
            job.run(s, d)
        else:
            pl.when(step < job.n_blocks)(functools.partial(job.run, s, d))


def _cast_kernel(*refs, jobs):
    n = len(jobs)
    for job, s, d in zip(jobs, refs[:n], refs[n:]):
        job.run(s, d)


def _cast_call(jobs, n_steps):
    return pl.pallas_call(
        functools.partial(_cast_kernel, jobs=jobs),
        out_shape=tuple(j.out_shape for j in jobs),
        grid=(n_steps,),
        in_specs=[j.in_spec for j in jobs],
        out_specs=tuple(j.out_spec for j in jobs),
        compiler_params=pltpu.CompilerParams(dimension_semantics=("arbitrary",)),
        name="cast_weights",
    )(*[j.src for j in jobs])


def _rope_tables(pos4, inv_row):
    ang = pos4 * inv_row
    lane = lax.broadcasted_iota(jnp.int32, (TILE // 4, LANES), 1)
    group = lane // HALF
    sign = jnp.where((lane % HEAD_DIM) < HALF, -1.0, 1.0)
    out = []
    for dense, sgn in ((jnp.cos(ang), None), (jnp.sin(ang), sign)):
        rolled = [dense] + [pltpu.roll(dense, HALF * k, 1) for k in range(1, 4)]
        slabs = []
        for a in range(4):
            e = rolled[(0 - a) % 4]
            for g in range(1, 4):
                e = jnp.where(group == g, rolled[(g - a) % 4], e)
            slabs.append(e if sgn is None else e * sgn)
        out.append(jnp.concatenate(slabs, axis=0))
    return out


def _qkv_kernel(*refs, n_casts, n_steps, jobs):
    x_refs = refs[:8]
    w_refs = refs[8:11]
    pos_ref, inv_ref = refs[11:13]
    cast_src = refs[13:13 + n_casts]
    out_refs = refs[13 + n_casts:22 + n_casts]
    cast_dst = refs[22 + n_casts:]
    _run_casts(jobs, cast_src, cast_dst, pl.program_id(0), n_steps)

    scale = HEAD_DIM ** -0.5 * LOG2_E
    first_half = (lax.broadcasted_iota(jnp.int32, (TILE, LANES), 1) % HEAD_DIM) < HALF

    for h in range(QKV_SUB):
        rows = [jnp.concatenate([xr[pl.ds(TILE * h + rho, ROWS_PER_RES, stride=N_RES), :] for xr in x_refs], axis=1)
                for rho in range(N_RES)]
        xp = jnp.concatenate(rows, axis=0).astype(BF16)
        cos, sin = _rope_tables(pos_ref[TILE // 4 * h:TILE // 4 * (h + 1), :], inv_ref[...])
        tables = ((cos * scale, sin * scale), (cos, sin))

        for g in range(N_GROUPS):
            for j in range(3):
                y = jnp.dot(xp, w_refs[g][:, GROUP_COL * j:GROUP_COL * (j + 1)], preferred_element_type=F32)
                if j < 2:
                    c, s = tables[j]
                    parts = []
                    for sl in range(GROUP_COL // LANES):
                        a = y[:, LANES * sl:LANES * (sl + 1)]
                        partner = jnp.where(first_half, pltpu.roll(a, LANES - HALF, 1), pltpu.roll(a, HALF, 1))
                        parts.append(a * c + partner * s)
                    y = jnp.concatenate(parts, axis=1)
                o_ref = out_refs[3 * g + j]
                if g == 0:
                    pieces = [y[32 * rho + 8 * c:32 * rho + 8 * c + 8, :] for c in range(4) for rho in range(N_RES)]
                    o_ref[TILE * h:TILE * (h + 1), :] = jnp.concatenate(pieces, axis=0).astype(BF16)
                elif g == 1:
                    for a in range(4):
                        slab = jnp.concatenate([y[32 * (4 * b + a):32 * (4 * b + a) + 32, :] for b in range(4)],
                                               axis=0)
                        o_ref[ATTN_BLK * h:ATTN_BLK * (h + 1), GROUP_COL * a:GROUP_COL * (a + 1)] = slab.astype(BF16)
                else:
                    for rho in range(N_RES):
                        o_ref[ROWS_PER_RES * h:ROWS_PER_RES * (h + 1), GROUP_COL * rho:GROUP_COL * (rho + 1)] = (
                            y[32 * rho:32 * rho + 32, :].astype(BF16))


def _qkv_call(x2d, w_groups, pos4, inv_row, jobs):
    t = x2d.shape[0]
    step_rows = QKV_SUB * TILE
    nt = t // step_rows
    x_specs = [pl.BlockSpec((step_rows, LANES), functools.partial(lambda i, c: (i, c), c=c)) for c in range(8)]
    out_shapes, out_specs = [], []
    for d in DILATIONS:
        for _ in range(3):
            out_shapes.append(jax.ShapeDtypeStruct((t // d, d * GROUP_COL), BF16))
            out_specs.append(pl.BlockSpec((step_rows // d, d * GROUP_COL), lambda i: (i, 0)))
    outs = pl.pallas_call(
        functools.partial(_qkv_kernel, n_casts=len(jobs), n_steps=nt, jobs=jobs),
        out_shape=tuple(out_shapes) + tuple(j.out_shape for j in jobs),
        grid=(nt,),
        in_specs=x_specs + [_resident(w.shape) for w in w_groups]
                 + [pl.BlockSpec((step_rows // 4, LANES), lambda i: (i, 0)), _resident(inv_row.shape)]
                 + [j.in_spec for j in jobs],
        out_specs=tuple(out_specs) + tuple(j.out_spec for j in jobs),
        compiler_params=pltpu.CompilerParams(dimension_semantics=("arbitrary",), vmem_limit_bytes=VMEM_LIMIT),
        name="qkv_rope",
    )(*([x2d] * 8), *w_groups, pos4, inv_row, *[j.src for j in jobs])
    return outs[:9], outs[9:]


def _mask16(cond):
    return jnp.where(cond, 1.0, 0.0).astype(BF16) > 0


def _attn_block(q, keys_of, vals_of, bias_t, state_get, state_put, emit):
    lo16 = _mask16(lax.broadcasted_iota(jnp.int32, (ATTN_BLK, LANES), 1) < HEAD_DIM)
    lo = lax.broadcasted_iota(jnp.int32, (1, LANES), 1) < HEAD_DIM
    eye = jnp.where(lax.broadcasted_iota(jnp.int32, (2 * ATTN_BLK, LANES), 0) % ATTN_BLK
                    == lax.broadcasted_iota(jnp.int32, (2 * ATTN_BLK, LANES), 1), 1.0, 0.0).astype(BF16)
    for j in range(HEADS // 2):
        sl = slice(LANES * j, LANES * (j + 1))
        qj = q[:, sl]
        zero = jnp.zeros_like(qj)
        lhs = jnp.concatenate([jnp.where(lo16, qj, zero), jnp.where(lo16, zero, qj)], axis=0)
        scores = lax.dot_general(jnp.concatenate([lhs, eye], axis=1),
                                 jnp.concatenate([keys_of(sl), bias_t], axis=1),
                                 (((1,), (1,)), ((), ())), preferred_element_type=F32)
        vals = vals_of(sl)
        rhs = jnp.concatenate([vals, jnp.ones_like(vals)], axis=1)
        outs, maxes = [], []
        for hl in range(2):
            s = scores[ATTN_BLK * hl:ATTN_BLK * (hl + 1), :]
            mb = jnp.max(s, axis=1, keepdims=True)
            p = jnp.exp2(s - mb).astype(BF16)
            outs.append(jnp.dot(p, rhs, preferred_element_type=F32))
            maxes.append(mb)
        o_new = jnp.where(lo, outs[0][:, :LANES], outs[1][:, :LANES])
        l_new = jnp.where(lo, outs[0][:, LANES:], outs[1][:, LANES:])
        m_new = jnp.where(lo, maxes[0], maxes[1])
        old = state_get(j)
        if old is not None:
            acc0, l0, m0 = old
            m = jnp.maximum(m0, m_new)
            a0 = jnp.exp2(m0 - m)
            a1 = jnp.exp2(m_new - m)
            o_new = acc0 * a0 + o_new * a1
            l_new = l0 * a0 + l_new * a1
            m_new = m
        if emit is not None:
            emit(j, o_new / l_new)
        else:
            state_put(j, o_new, l_new, m_new)


def _attn_kernel(q1, k1, v1, kp1, vp1, q2, k2, v2, kp2, vp2, q3, k3, v3, kp3, vp3, mask_ref, out_ref,
                 acc_s, l_s, m_s):
    n = pl.program_id(1)
    s = pl.program_id(2)
    state = (acc_s, l_s, m_s)

    def lanes(j):
        return slice(LANES * j, LANES * (j + 1))

    def bias_of(g, first):
        return mask_ref[g, 0] if first is None else mask_ref[g, first.astype(jnp.int32)]

    def two_blocks(prev_ref, cur_ref, col0):
        def load(sl):
            cols = slice(col0 + sl.start, col0 + sl.stop)
            return jnp.concatenate([prev_ref[:, cols], cur_ref[:ATTN_BLK, cols]], axis=0)
        return load

    spg = _STEPS_PER_GROUP

    @pl.when(s < spg)
    def _group1():
        for c in range(STEP_BLOCKS):
            k_blk = pl.multiple_of((STEP_BLOCKS * s + c) * SUBLANES, SUBLANES)

            def put(j, *vals, k_blk=k_blk):
                for ref, val in zip(state, vals):
                    for rho in range(N_RES):
                        ref[rho, pl.ds(k_blk, SUBLANES), lanes(j)] = val[SUBLANES * rho:SUBLANES * (rho + 1), :]

            rows = slice(ATTN_BLK * c, ATTN_BLK * (c + 1))
            if c == 0:
                keys_of = lambda sl: jnp.concatenate([kp1[:, sl], k1[:ATTN_BLK, sl]], axis=0)
                vals_of = lambda sl: jnp.concatenate([vp1[:, sl], v1[:ATTN_BLK, sl]], axis=0)
                first = jnp.logical_and(n == 0, s == 0)
            else:
                both = slice(ATTN_BLK * (c - 1), ATTN_BLK * (c + 1))
                keys_of = lambda sl, both=both: k1[both, sl]
                vals_of = lambda sl, both=both: v1[both, sl]
                first = None
            _attn_block(q1[rows, :], keys_of, vals_of, bias_of(0, first), lambda j: None, put, None)

    @pl.when(jnp.logical_and(s >= spg, s < 2 * spg))
    def _group2():
        t = s - spg
        for kk in range(STEP_BLOCKS // 4):
            base = pl.multiple_of((t * (STEP_BLOCKS // 4) + kk) * 32, 32)
            bias = bias_of(1, jnp.logical_and(n == 0, t == 0)) if kk == 0 else bias_of(1, None)
            rows = slice(ATTN_BLK * kk, ATTN_BLK * (kk + 1))
            both = slice(ATTN_BLK * (kk - 1), ATTN_BLK * (kk + 1))
            for a in range(4):
                def get(j, a=a, base=base):
                    return tuple(jnp.concatenate([ref[4 * b + a, pl.ds(base, 32), lanes(j)] for b in range(4)],
                                                 axis=0) for ref in state)

                def put(j, *vals, a=a, base=base):
                    for ref, val in zip(state, vals):
                        for b in range(4):
                            ref[4 * b + a, pl.ds(base, 32), lanes(j)] = val[32 * b:32 * (b + 1), :]

                col0 = GROUP_COL * a
                if kk == 0:
                    keys_of = two_blocks(kp2, k2, col0)
                    vals_of = two_blocks(vp2, v2, col0)
                else:
                    keys_of = lambda sl, col0=col0, both=both: k2[both, col0 + sl.start:col0 + sl.stop]
                    vals_of = lambda sl, col0=col0, both=both: v2[both, col0 + sl.start:col0 + sl.stop]
                _attn_block(q2[rows, col0:col0 + GROUP_COL], keys_of, vals_of, bias, get, put, None)

    @pl.when(s >= 2 * spg)
    def _group3():
        t = s - 2 * spg
        bias = bias_of(2, n == 0)
        for c in range(STEP_BLOCKS):
            rho = STEP_BLOCKS * t + c

            def get(j, rho=rho):
                return tuple(ref[rho, :, lanes(j)] for ref in state)

            def emit(j, y, c=c):
                out_ref[0, c, :, lanes(j)] = y.astype(BF16)

            col0 = GROUP_COL * c
            _attn_block(q3[:, col0:col0 + GROUP_COL], two_blocks(kp3, k3, col0), two_blocks(vp3, v3, col0),
                        bias, get, None, emit)


def _attn_call(qkv, masks, batch, seq):
    nchunk = seq // CHUNK
    spg = _STEPS_PER_GROUP
    rows1 = STEP_BLOCKS * ATTN_BLK
    rows2 = (STEP_BLOCKS // 4) * ATTN_BLK

    def step(s, g):
        return jnp.clip(s - spg * g, 0, spg - 1)

    def cur1(b, n, s):
        return (b * (seq // rows1) + spg * n + step(s, 0), 0)

    def prev1(b, n, s):
        return (jnp.maximum(STEP_BLOCKS * cur1(b, n, s)[0] - 1, 0), 0)

    def cur2(b, n, s):
        return (b * (seq // 4 // rows2) + spg * n + step(s, 1), 0)

    def prev2(b, n, s):
        return (jnp.maximum((STEP_BLOCKS // 4) * cur2(b, n, s)[0] - 1, 0), 0)

    def cur3(b, n, s):
        return (b * nchunk + n, step(s, 2))

    def prev3(b, n, s):
        return (jnp.maximum(b * nchunk + n - 1, 0), step(s, 2))

    blk1 = (rows1, GROUP_COL)
    blk2 = (rows2, 4 * GROUP_COL)
    prev2_blk = (ATTN_BLK, 4 * GROUP_COL)
    blk3 = (ATTN_BLK, STEP_BLOCKS * GROUP_COL)
    in_specs = [
        pl.BlockSpec(blk1, cur1), pl.BlockSpec(blk1, cur1), pl.BlockSpec(blk1, cur1),
        pl.BlockSpec((ATTN_BLK, GROUP_COL), prev1), pl.BlockSpec((ATTN_BLK, GROUP_COL), prev1),
        pl.BlockSpec(blk2, cur2), pl.BlockSpec(blk2, cur2), pl.BlockSpec(blk2, cur2),
        pl.BlockSpec(prev2_blk, prev2), pl.BlockSpec(prev2_blk, prev2),
        pl.BlockSpec(blk3, cur3), pl.BlockSpec(blk3, cur3), pl.BlockSpec(blk3, cur3),
        pl.BlockSpec(blk3, prev3), pl.BlockSpec(blk3, prev3),
        _resident(masks.shape),
    ]
    q1, k1, v1, q2, k2, v2, q3, k3, v3 = qkv
    return pl.pallas_call(
        _attn_kernel,
        out_shape=jax.ShapeDtypeStruct((batch, N_RES, seq // N_RES, GROUP_COL), BF16),
        grid=(batch, nchunk, N_GROUPS * spg),
        in_specs=in_specs,
        out_specs=pl.BlockSpec((1, STEP_BLOCKS, ATTN_BLK, GROUP_COL), lambda b, n, s: (b, step(s, 2), n, 0)),
        scratch_shapes=[pltpu.VMEM((N_RES, ATTN_BLK, GROUP_COL), F32)] * 3,
        compiler_params=pltpu.CompilerParams(dimension_semantics=("arbitrary",) * 3, vmem_limit_bytes=VMEM_LIMIT),
        name="dilated_attention",
    )(q1, k1, v1, k1, v1, q2, k2, v2, k2, v2, q3, k3, v3, k3, v3, masks)


def _mix_kernel(*refs, tiles_per_batch, n_casts, n_steps, jobs):
    x_ref, y_ref, w_ref, vec_ref, ws_ref, bsb_ref, pa_ref, pb_ref, pc_ref, wo_ref = refs[:10]
    cast_src = refs[10:10 + n_casts]
    o_ref = refs[10 + n_casts]
    cast_dst = refs[11 + n_casts:11 + 2 * n_casts]
    zc_ref, ys_ref = refs[11 + 2 * n_casts:]
    i = pl.program_id(0)

    @pl.when(i % tiles_per_batch == 0)
    def _():
        zc_ref[...] = jnp.zeros_like(zc_ref)

    _run_casts(jobs, cast_src, cast_dst, i, n_steps)

    x = x_ref[...]
    xb = x.astype(BF16)

    def vec(r):
        return vec_ref[r:r + 1, :]

    def proj(k):
        return jnp.dot(xb, w_ref[:, D_MODEL * k:D_MODEL * (k + 1)], preferred_element_type=F32)

    def out_proj(a, p_ref):
        return jnp.dot(a, p_ref[:, :D_MODEL], preferred_element_type=F32)

    v_pre = proj(7)
    u_pre = proj(6)

    for rho in range(N_RES):
        piece = y_ref[0, rho].astype(F32)
        for sl in range(GROUP_COL // LANES):
            ys_ref[sl, pl.ds(rho, ROWS_PER_RES, stride=N_RES), :] = piece[:, LANES * sl:LANES * (sl + 1)]
    yb = jnp.concatenate([ys_ref[sl] for sl in range(GROUP_COL // LANES)], axis=1).astype(BF16)
    m_attn = jax.nn.sigmoid(proj(1)) * out_proj(yb, pb_ref)

    z = proj(4) * proj(5)
    head = jnp.concatenate([zc_ref[...], z[:SUBLANES, :]], axis=0)
    zc_ref[...] = z[TILE - SUBLANES:, :]
    conv = vec(_P_CONV + 2) * z
    for shift in (1, 2):
        zs = jnp.concatenate([pltpu.roll(head, shift, 0)[SUBLANES:, :], pltpu.roll(z, shift, 0)[SUBLANES:, :]], axis=0)
        conv = conv + vec(_P_CONV + 2 - shift) * zs
    ya = (proj(3) * conv).astype(BF16)
    m = m_attn + jax.nn.sigmoid(proj(0)) * out_proj(ya, pa_ref)

    u = _gelu(u_pre)
    vb = _layer_norm(_gelu(v_pre), vec(_P_GLN_G), vec(_P_GLN_B)).astype(BF16)
    n_chunks = TILE // GMLP_CHUNK
    tril = (lax.broadcasted_iota(jnp.int32, (GMLP_CHUNK, GMLP_CHUNK), 0)
            >= lax.broadcasted_iota(jnp.int32, (GMLP_CHUNK, GMLP_CHUNK), 1))
    cols = []
    for g in range(GMLP_GROUPS):
        gl = slice(LANES * g, LANES * (g + 1))
        wg = jnp.where(tril, ws_ref[g], 0.0).astype(BF16)
        rhs = jnp.concatenate([vb[GMLP_CHUNK * c:GMLP_CHUNK * (c + 1), gl] for c in range(n_chunks)], axis=1)
        sp = jnp.dot(wg, rhs, preferred_element_type=F32)
        bias = bsb_ref[:, gl]
        cols.append(jnp.concatenate([sp[:, LANES * c:LANES * (c + 1)] + bias for c in range(n_chunks)], axis=0))
    yc = (u * jnp.concatenate(cols, axis=1)).astype(BF16)
    m = m + jax.nn.sigmoid(proj(2)) * out_proj(yc, pc_ref)

    mb16 = m.astype(BF16)
    for r in range(2):
        rs = slice(TILE // 2 * r, TILE // 2 * (r + 1))
        mixed = out_proj(mb16[rs, :], wo_ref)
        o_ref[rs, :] = _layer_norm(ALPHA * x[rs, :] + mixed, vec(_P_LN1_G), vec(_P_LN1_B))


def _mix_call(x2d, y_attn, w_rest, vecs, w_s, bsb, pa, pb, pc, wo, layer, seq, jobs):
    t = x2d.shape[0]
    nt = t // TILE
    tiles_per_batch = seq // TILE
    outs = pl.pallas_call(
        functools.partial(_mix_kernel, tiles_per_batch=tiles_per_batch, n_casts=len(jobs), n_steps=nt, jobs=jobs),
        out_shape=(jax.ShapeDtypeStruct((t, D_MODEL), F32),) + tuple(j.out_shape for j in jobs),
        grid=(nt,),
        in_specs=[pl.BlockSpec((TILE, D_MODEL), lambda i: (i, 0)),
                  pl.BlockSpec((1, N_RES, ROWS_PER_RES, GROUP_COL),
                               lambda i: (i // tiles_per_batch, 0, i % tiles_per_batch, 0)),
                  _resident(w_rest.shape), _layer_resident(vecs.shape, layer), _layer_resident(w_s.shape, layer),
                  _layer_resident(bsb.shape, layer)]
                 + [_resident(w.shape) for w in (pa, pb, pc, wo)] + [j.in_spec for j in jobs],
        out_specs=(pl.BlockSpec((TILE, D_MODEL), lambda i: (i, 0)),) + tuple(j.out_spec for j in jobs),
        scratch_shapes=[pltpu.VMEM((SUBLANES, D_MODEL), F32),
                        pltpu.VMEM((GROUP_COL // LANES, TILE, LANES), F32)],
        compiler_params=pltpu.CompilerParams(dimension_semantics=("arbitrary",), vmem_limit_bytes=VMEM_LIMIT),
        name="mixers_out_ln",
    )(x2d, y_attn, w_rest, vecs, w_s, bsb, pa, pb, pc, wo, *[j.src for j in jobs])
    return outs[0], outs[1:]


def _ffn_kernel(*refs, n_casts, n_steps, jobs):
    x_ref, wg_ref, wu_ref, wd_ref, vec_ref = refs[:5]
    cast_src = refs[5:5 + n_casts]
    o_ref = refs[5 + n_casts]
    cast_dst = refs[6 + n_casts:]
    _run_casts(jobs, cast_src, cast_dst, pl.program_id(0), n_steps)

    x = x_ref[...]
    xb = x.astype(BF16)
    gate = jnp.dot(xb, wg_ref[...], preferred_element_type=F32)
    up = jnp.dot(xb, wu_ref[...], preferred_element_type=F32)
    h = (jax.nn.silu(gate) * up).astype(BF16)
    for r in range(4):
        rs = slice(FFN_ROWS // 4 * r, FFN_ROWS // 4 * (r + 1))
        down = jnp.dot(h[rs, :], wd_ref[:, :D_MODEL], preferred_element_type=F32)
        o_ref[rs, :] = _layer_norm(ALPHA * x[rs, :] + down, vec_ref[_P_LN2_G:_P_LN2_G + 1, :],
                                   vec_ref[_P_LN2_B:_P_LN2_B + 1, :])


def _ffn_call(x2d, wg, wu, wd, vecs, layer, jobs):
    t = x2d.shape[0]
    nt = t // FFN_ROWS
    outs = pl.pallas_call(
        functools.partial(_ffn_kernel, n_casts=len(jobs), n_steps=nt, jobs=jobs),
        out_shape=(jax.ShapeDtypeStruct((t, D_MODEL), F32),) + tuple(j.out_shape for j in jobs),
        grid=(nt,),
        in_specs=[pl.BlockSpec((FFN_ROWS, D_MODEL), lambda i: (i, 0))] + [_resident(w.shape) for w in (wg, wu, wd)]
                 + [_layer_resident(vecs.shape, layer)] + [j.in_spec for j in jobs],
        out_specs=(pl.BlockSpec((FFN_ROWS, D_MODEL), lambda i: (i, 0)),) + tuple(j.out_spec for j in jobs),
        compiler_params=pltpu.CompilerParams(dimension_semantics=("arbitrary",), vmem_limit_bytes=VMEM_LIMIT),
        name="swiglu_ln",
    )(x2d, wg, wu, wd, vecs, *[j.src for j in jobs])
    return outs[0], outs[1:]


def _band_masks():
    u = np.arange(ATTN_BLK)
    stored = (
        (u % N_RES) * SUBLANES + u // N_RES,
        (u % 4) * 32 + u // 4,
        u,
    )
    out = np.empty((N_GROUPS, 2, 2 * ATTN_BLK, ATTN_BLK), np.float32)
    for g, st in enumerate(stored):
        orig = np.empty(ATTN_BLK, np.int64)
        orig[st] = u
        ki = orig[:, None]
        qi = orig[None, :]
        out[g, 0, :ATTN_BLK] = np.where(ki >= qi, 0.0, NEG)
        out[g, 1, :ATTN_BLK] = NEG
        out[g, :, ATTN_BLK:] = np.where(ki <= qi, 0.0, NEG)
    return out


def kernel(x, positions, w_in, conv_w, gmlp_ln_g, gmlp_ln_b, w_s, b_s, p_a, p_b, p_c, w_o, ln1_g, ln1_b, w_gate, w_up,
           w_down, ln2_g, ln2_b):
    batch, seq, d = x.shape
    t = batch * seq
    nt = t // TILE
    depth = w_in.shape[0]
    n_cast = _CAST_BLOCKS
    assert d == D_MODEL and seq % CHUNK == 0 and depth == DEPTH
    assert min(t // (QKV_SUB * TILE), t // FFN_ROWS, nt) >= n_cast

    pos = positions.reshape(nt, ROWS_PER_RES, N_RES).transpose(0, 2, 1).reshape(nt, 4, TILE // 4).astype(F32)
    pos4 = jnp.repeat(pos.transpose(0, 2, 1), HALF, axis=2).reshape(t // 4, LANES)
    inv_freq = ROPE_THETA ** (-jnp.arange(HALF, dtype=F32) / HALF)
    inv_row = jnp.tile(inv_freq, LANES // HALF).reshape(1, LANES)

    masks = jnp.asarray(_band_masks(), dtype=BF16)
    vecs = jnp.concatenate([conv_w] + [v[:, None, :] for v in (gmlp_ln_g, gmlp_ln_b, ln1_g, ln1_b, ln2_g, ln2_b)],
                           axis=1)
    bsb = jnp.repeat(jnp.swapaxes(b_s, 1, 2), GMLP_CHUNK, axis=2)

    rest_bw = _REST_COLS // n_cast
    gap = (_OFF_GMLP - _OFF_ATTN) // rest_bw

    def rest_col_block(i):
        return jnp.where(i < _OFF_ATTN // rest_bw, i, i + gap)

    def qkv_jobs(layer, n_blocks):
        return [_CastJob(w_in, layer, D_MODEL, _QKV_COLS, n_blocks, col_block=_OFF_ATTN // _QKV_COLS + g)
                for g in range(N_GROUPS)]

    w_groups = _cast_call(qkv_jobs(0, SUBLANES), SUBLANES)
    h = x.reshape(t, D_MODEL)
    for l in range(depth):
        jobs = [_CastJob(w_in, l, D_MODEL, _REST_COLS, n_cast, by_cols=True, col_map=rest_col_block),
                _CastJob(p_a, l, D_MODEL, D_MODEL, n_cast), _CastJob(p_b, l, GROUP_COL, D_MODEL, n_cast),
                _CastJob(p_c, l, D_MODEL, D_MODEL, n_cast), _CastJob(w_o, l, D_MODEL, D_MODEL, n_cast),
                _CastJob(w_down, l, D_FF, D_MODEL, n_cast)]
        qkv, (w_rest, pa, pb, pc, wo, wd) = _qkv_call(h, w_groups, pos4, inv_row, jobs)
        y_attn = _attn_call(qkv, masks, batch, seq)
        jobs = [_CastJob(w_gate, l, D_MODEL, D_FF, nt), _CastJob(w_up, l, D_MODEL, D_FF, nt)]
        h, (wg, wu) = _mix_call(h, y_attn, w_rest, vecs, w_s, bsb, pa, pb, pc, wo, l, seq, jobs)
        jobs = qkv_jobs(l + 1, n_cast) if l + 1 < depth else []
        h, w_next = _ffn_call(h, wg, wu, wd, vecs, l, jobs)
        w_groups = tuple(w_next)
    return h.reshape(batch, seq, D_MODEL)
```

```python
import functools

import numpy as np
import jax
import jax.numpy as jnp
from jax import lax
from jax.experimental import pallas as pl
from jax.experimental.pallas import tpu as pltpu

F32 = jnp.float32
BF16 = jnp.bfloat16

D_MODEL = 1024
HEADS = 8
HEAD_DIM = 64
HALF = HEAD_DIM // 2
GROUP_COL = HEADS * HEAD_DIM
DILATIONS = (1, 4, 16)
N_GROUPS = len(DILATIONS)
ATTN_BLK = 128
N_RES = 16
CHUNK = N_RES * ATTN_BLK
STEP_BLOCKS = 8
_STEPS_PER_GROUP = N_RES // STEP_BLOCKS
GMLP_GROUPS = 8
GMLP_CHUNK = 128
D_FF = 2816
DEPTH = 2
ALPHA = (2 * DEPTH) ** 0.25
LN_EPS = 1e-5
ROPE_THETA = 10000.0
LOG2_E = 1.4426950408889634

TILE = 512
QKV_SUB = 2
FFN_ROWS = 1024
ROWS_PER_RES = TILE // N_RES
LANES = 128
SUBLANES = 8
NEG = -1e30
VMEM_LIMIT = 56 * 1024 * 1024

_OFF_ATTN = 6 * D_MODEL
_QKV_COLS = 3 * GROUP_COL
_OFF_GMLP = _OFF_ATTN + N_GROUPS * _QKV_COLS
_REST_COLS = _OFF_ATTN + 2 * D_MODEL
_CAST_BLOCKS = 16

_P_CONV, _P_GLN_G, _P_GLN_B, _P_LN1_G, _P_LN1_B, _P_LN2_G, _P_LN2_B = 0, 3, 4, 5, 6, 7, 8


def _resident(shape):
    nd = len(shape)
    return pl.BlockSpec(shape, lambda *_: (0,) * nd, pipeline_mode=pl.Buffered(1))


def _layer_resident(shape, layer):
    nd = len(shape)
    return pl.BlockSpec((None,) + tuple(shape[1:]), lambda *_: (layer,) + (0,) * (nd - 1),
                        pipeline_mode=pl.Buffered(1))


def _layer_norm(v, g, b):
    mu = jnp.mean(v, axis=-1, keepdims=True)
    c = v - mu
    var = jnp.mean(c * c, axis=-1, keepdims=True)
    return c * lax.rsqrt(var + LN_EPS) * g + b


def _gelu(v):
    return 0.5 * v * (1.0 + lax.erf(v * 0.7071067811865476))


class _CastJob:
    def __init__(self, src, layer, rows, cols, n_blocks, *, col_block=0, by_cols=False, col_map=None):
        self.src, self.n_blocks, self.cols = src, n_blocks, cols
        last = n_blocks - 1
        if by_cols:
            bw = cols // n_blocks
            self.in_spec = pl.BlockSpec((None, rows, bw), lambda i: (layer, 0, col_map(jnp.minimum(i, last))))
            self.out_spec = pl.BlockSpec((rows, bw), lambda i: (0, jnp.minimum(i, last)))
            self.out_shape = jax.ShapeDtypeStruct((rows, cols), BF16)
        else:
            rb = rows // n_blocks
            out_cols = cols + LANES if cols == D_MODEL else cols
            self.in_spec = pl.BlockSpec((None, rb, cols), lambda i: (layer, jnp.minimum(i, last), col_block))
            self.out_spec = pl.BlockSpec((rb, out_cols), lambda i: (jnp.minimum(i, last), 0))
            self.out_shape = jax.ShapeDtypeStruct((rows, out_cols), BF16)

    def run(self, s, d):
        if d.shape[1] == s.shape[1]:
            d[...] = s[...].astype(BF16)
        else:
            d[:, :self.cols] = s[...].astype(BF16)
            d[:, self.cols:] = jnp.zeros((d.shape[0], d.shape[1] - self.cols), BF16)


def _run_casts(jobs, src_refs, dst_refs, step, n_steps):
    for job, s, d in zip(jobs, src_refs, dst_refs):
        if job.n_blocks >= n_steps:
            job.run(s, d)
        else:
            pl.when(step < job.n_blocks)(functools.partial(job.run, s, d))


def _cast_kernel(*refs, jobs):
    n = len(jobs)
    for job, s, d in zip(jobs, refs[:n], refs[n:]):
        job.run(s, d)


def _cast_call(jobs, n_steps):
    return pl.pallas_call(
        functools.partial(_cast_kernel, jobs=jobs),
        out_shape=tuple(j.out_shape for j in jobs),
        grid=(n_steps,),
        in_specs=[j.in_spec for j in jobs],
        out_specs=tuple(j.out_spec for j in jobs),
        compiler_params=pltpu.CompilerParams(dimension_semantics=("arbitrary",)),
        name="cast_weights",
    )(*[j.src for j in jobs])


def _rope_tables(pos4, inv_row):
    ang = pos4 * inv_row
    lane = lax.broadcasted_iota(jnp.int32, (TILE // 4, LANES), 1)
    group = lane // HALF
    sign = jnp.where((lane % HEAD_DIM) < HALF, -1.0, 1.0)
    out = []
    for dense, sgn in ((jnp.cos(ang), None), (jnp.sin(ang), sign)):
        rolled = [dense] + [pltpu.roll(dense, HALF * k, 1) for k in range(1, 4)]
        slabs = []
        for a in range(4):
            e = rolled[(0 - a) % 4]
            for g in range(1, 4):
                e = jnp.where(group == g, rolled[(g - a) % 4], e)
            slabs.append(e if sgn is None else e * sgn)
        out.append(jnp.concatenate(slabs, axis=0))
    return out


def _qkv_kernel(*refs, n_casts, n_steps, jobs):
    x_refs = refs[:8]
    w_refs = refs[8:11]
    pos_ref, inv_ref = refs[11:13]
    cast_src = refs[13:13 + n_casts]
    out_refs = refs[13 + n_casts:22 + n_casts]
    cast_dst = refs[22 + n_casts:]
    _run_casts(jobs, cast_src, cast_dst, pl.program_id(0), n_steps)

    scale = HEAD_DIM ** -0.5 * LOG2_E
    first_half = (lax.broadcasted_iota(jnp.int32, (TILE, LANES), 1) % HEAD_DIM) < HALF

    for h in range(QKV_SUB):
        rows = [jnp.concatenate([xr[pl.ds(TILE * h + rho, ROWS_PER_RES, stride=N_RES), :] for xr in x_refs], axis=1)
                for rho in range(N_RES)]
        xp = jnp.concatenate(rows, axis=0).astype(BF16)
        cos, sin = _rope_tables(pos_ref[TILE // 4 * h:TILE // 4 * (h + 1), :], inv_ref[...])
        tables = ((cos * scale, sin * scale), (cos, sin))

        for g in range(N_GROUPS):
            for j in range(3):
                y = jnp.dot(xp, w_refs[g][:, GROUP_COL * j:GROUP_COL * (j + 1)], preferred_element_type=F32)
                if j < 2:
                    c, s = tables[j]
                    parts = []
                    for sl in range(GROUP_COL // LANES):
                        a = y[:, LANES * sl:LANES * (sl + 1)]
                        partner = jnp.where(first_half, pltpu.roll(a, LANES - HALF, 1), pltpu.roll(a, HALF, 1))
                        parts.append(a * c + partner * s)
                    y = jnp.concatenate(parts, axis=1)
                o_ref = out_refs[3 * g + j]
                if g == 0:
                    pieces = [y[32 * rho + 8 * c:32 * rho + 8 * c + 8, :] for c in range(4) for rho in range(N_RES)]
                    o_ref[TILE * h:TILE * (h + 1), :] = jnp.concatenate(pieces, axis=0).astype(BF16)
                elif g == 1:
                    for a in range(4):
                        slab = jnp.concatenate([y[32 * (4 * b + a):32 * (4 * b + a) + 32, :] for b in range(4)],
                                               axis=0)
                        o_ref[ATTN_BLK * h:ATTN_BLK * (h + 1), GROUP_COL * a:GROUP_COL * (a + 1)] = slab.astype(BF16)
                else:
                    for rho in range(N_RES):
                        o_ref[ROWS_PER_RES * h:ROWS_PER_RES * (h + 1), GROUP_COL * rho:GROUP_COL * (rho + 1)] = (
                            y[32 * rho:32 * rho + 32, :].astype(BF16))


def _qkv_call(x2d, w_groups, pos4, inv_row, jobs):
    t = x2d.shape[0]
    step_rows = QKV_SUB * TILE
    nt = t // step_rows
    x_specs = [pl.BlockSpec((step_rows, LANES), functools.partial(lambda i, c: (i, c), c=c)) for c in range(8)]
    out_shapes, out_specs = [], []
    for d in DILATIONS:
        for _ in range(3):
            out_shapes.append(jax.ShapeDtypeStruct((t // d, d * GROUP_COL), BF16))
            out_specs.append(pl.BlockSpec((step_rows // d, d * GROUP_COL), lambda i: (i, 0)))
    outs = pl.pallas_call(
        functools.partial(_qkv_kernel, n_casts=len(jobs), n_steps=nt, jobs=jobs),
        out_shape=tuple(out_shapes) + tuple(j.out_shape for j in jobs),
        grid=(nt,),
        in_specs=x_specs + [_resident(w.shape) for w in w_groups]
                 + [pl.BlockSpec((step_rows // 4, LANES), lambda i: (i, 0)), _resident(inv_row.shape)]
                 + [j.in_spec for j in jobs],
        out_specs=tuple(out_specs) + tuple(j.out_spec for j in jobs),
        compiler_params=pltpu.CompilerParams(dimension_semantics=("arbitrary",), vmem_limit_bytes=VMEM_LIMIT),
        name="qkv_rope",
    )(*([x2d] * 8), *w_groups, pos4, inv_row, *[j.src for j in jobs])
    return outs[:9], outs[9:]


def _mask16(cond):
    return jnp.where(cond, 1.0, 0.0).astype(BF16) > 0


def _attn_block(q, keys_of, vals_of, bias_t, state_get, state_put, emit):
    lo16 = _mask16(lax.broadcasted_iota(jnp.int32, (ATTN_BLK, LANES), 1) < HEAD_DIM)
    lo = lax.broadcasted_iota(jnp.int32, (1, LANES), 1) < HEAD_DIM
    eye = jnp.where(lax.broadcasted_iota(jnp.int32, (2 * ATTN_BLK, LANES), 0) % ATTN_BLK
                    == lax.broadcasted_iota(jnp.int32, (2 * ATTN_BLK, LANES), 1), 1.0, 0.0).astype(BF16)
    for j in range(HEADS // 2):
        sl = slice(LANES * j, LANES * (j + 1))
        qj = q[:, sl]
        zero = jnp.zeros_like(qj)
        lhs = jnp.concatenate([jnp.where(lo16, qj, zero), jnp.where(lo16, zero, qj)], axis=0)
        scores = lax.dot_general(jnp.concatenate([lhs, eye], axis=1),
                                 jnp.concatenate([keys_of(sl), bias_t], axis=1),
                                 (((1,), (1,)), ((), ())), preferred_element_type=F32)
        vals = vals_of(sl)
        rhs = jnp.concatenate([vals, jnp.ones_like(vals)], axis=1)
        outs, maxes = [], []
        for hl in range(2):
            s = scores[ATTN_BLK * hl:ATTN_BLK * (hl + 1), :]
            mb = jnp.max(s, axis=1, keepdims=True)
            p = jnp.exp2(s - mb).astype(BF16)
            outs.append(jnp.dot(p, rhs, preferred_element_type=F32))
            maxes.append(mb)
        o_new = jnp.where(lo, outs[0][:, :LANES], outs[1][:, :LANES])
        l_new = jnp.where(lo, outs[0][:, LANES:], outs[1][:, LANES:])
        m_new = jnp.where(lo, maxes[0], maxes[1])
        old = state_get(j)
        if old is not None:
            acc0, l0, m0 = old
            m = jnp.maximum(m0, m_new)
            a0 = jnp.exp2(m0 - m)
            a1 = jnp.exp2(m_new - m)
            o_new = acc0 * a0 + o_new * a1
            l_new = l0 * a0 + l_new * a1
            m_new = m
        if emit is not None:
            emit(j, o_new / l_new)
        else:
            state_put(j, o_new, l_new, m_new)


def _attn_kernel(q1, k1, v1, kp1, vp1, q2, k2, v2, kp2, vp2, q3, k3, v3, kp3, vp3, mask_ref, out_ref,
                 acc_s, l_s, m_s):
    n = pl.program_id(1)
    s = pl.program_id(2)
    state = (acc_s, l_s, m_s)

    def lanes(j):
        return slice(LANES * j, LANES * (j + 1))

    def bias_of(g, first):
        return mask_ref[g, 0] if first is None else mask_ref[g, first.astype(jnp.int32)]

    def two_blocks(prev_ref, cur_ref, col0):
        def load(sl):
            cols = slice(col0 + sl.start, col0 + sl.stop)
            return jnp.concatenate([prev_ref[:, cols], cur_ref[:ATTN_BLK, cols]], axis=0)
        return load

    spg = _STEPS_PER_GROUP

    @pl.when(s == 0)
    def _group1():
        for c in range(N_RES):
            k_blk = c * SUBLANES

            def put(j, *vals, k_blk=k_blk):
                for ref, val in zip(state, vals):
                    for rho in range(N_RES):
                        ref[rho, pl.ds(k_blk, SUBLANES), lanes(j)] = val[SUBLANES * rho:SUBLANES * (rho + 1), :]

            rows = slice(ATTN_BLK * c, ATTN_BLK * (c + 1))
            if c == 0:
                keys_of = lambda sl: jnp.concatenate([kp1[:, sl], k1[:ATTN_BLK, sl]], axis=0)
                vals_of = lambda sl: jnp.concatenate([vp1[:, sl], v1[:ATTN_BLK, sl]], axis=0)
                first = n == 0
            else:
                both = slice(ATTN_BLK * (c - 1), ATTN_BLK * (c + 1))
                keys_of = lambda sl, both=both: k1[both, sl]
                vals_of = lambda sl, both=both: v1[both, sl]
                first = None
            _attn_block(q1[rows, :], keys_of, vals_of, bias_of(0, first), lambda j: None, put, None)

    @pl.when(jnp.logical_and(s >= 1, s < 1 + spg))
    def _group2():
        t = s - 1
        for kk in range(STEP_BLOCKS // 4):
            base = pl.multiple_of((t * (STEP_BLOCKS // 4) + kk) * 32, 32)
            bias = bias_of(1, jnp.logical_and(n == 0, t == 0)) if kk == 0 else bias_of(1, None)
            rows = slice(ATTN_BLK * kk, ATTN_BLK * (kk + 1))
            both = slice(ATTN_BLK * (kk - 1), ATTN_BLK * (kk + 1))
            for a in range(4):
                def get(j, a=a, base=base):
                    return tuple(jnp.concatenate([ref[4 * b + a, pl.ds(base, 32), lanes(j)] for b in range(4)],
                                                 axis=0) for ref in state)

                def put(j, *vals, a=a, base=base):
                    for ref, val in zip(state, vals):
                        for b in range(4):
                            ref[4 * b + a, pl.ds(base, 32), lanes(j)] = val[32 * b:32 * (b + 1), :]

                col0 = GROUP_COL * a
                if kk == 0:
                    keys_of = two_blocks(kp2, k2, col0)
                    vals_of = two_blocks(vp2, v2, col0)
                else:
                    keys_of = lambda sl, col0=col0, both=both: k2[both, col0 + sl.start:col0 + sl.stop]
                    vals_of = lambda sl, col0=col0, both=both: v2[both, col0 + sl.start:col0 + sl.stop]
                _attn_block(q2[rows, col0:col0 + GROUP_COL], keys_of, vals_of, bias, get, put, None)

    @pl.when(s >= 1 + spg)
    def _group3():
        t = s - 1 - spg
        bias = bias_of(2, n == 0)
        for c in range(STEP_BLOCKS):
            rho = STEP_BLOCKS * t + c

            def get(j, rho=rho):
                return tuple(ref[rho, :, lanes(j)] for ref in state)

            def emit(j, y, c=c):
                out_ref[0, c, :, lanes(j)] = y.astype(BF16)

            col0 = GROUP_COL * c
            _attn_block(q3[:, col0:col0 + GROUP_COL], two_blocks(kp3, k3, col0), two_blocks(vp3, v3, col0),
                        bias, get, None, emit)


def _attn_call(qkv, masks, batch, seq):
    nchunk = seq // CHUNK
    spg = _STEPS_PER_GROUP
    rows1 = CHUNK
    rows2 = (STEP_BLOCKS // 4) * ATTN_BLK

    def step(s, g):
        return jnp.clip(s - 1 - spg * (g - 1), 0, spg - 1)

    def cur1(b, n, s):
        return (b * nchunk + n, 0)

    def prev1(b, n, s):
        return (jnp.maximum(N_RES * (b * nchunk + n) - 1, 0), 0)

    def cur2(b, n, s):
        return (b * (seq // 4 // rows2) + spg * n + step(s, 1), 0)

    def prev2(b, n, s):
        return (jnp.maximum((STEP_BLOCKS // 4) * cur2(b, n, s)[0] - 1, 0), 0)

    def cur3(b, n, s):
        return (b * nchunk + n, step(s, 2))

    def prev3(b, n, s):
        return (jnp.maximum(b * nchunk + n - 1, 0), step(s, 2))

    blk1 = (rows1, GROUP_COL)
    blk2 = (rows2, 4 * GROUP_COL)
    prev2_blk = (ATTN_BLK, 4 * GROUP_COL)
    blk3 = (ATTN_BLK, STEP_BLOCKS * GROUP_COL)
    in_specs = [
        pl.BlockSpec(blk1, cur1), pl.BlockSpec(blk1, cur1), pl.BlockSpec(blk1, cur1),
        pl.BlockSpec((ATTN_BLK, GROUP_COL), prev1), pl.BlockSpec((ATTN_BLK, GROUP_COL), prev1),
        pl.BlockSpec(blk2, cur2), pl.BlockSpec(blk2, cur2), pl.BlockSpec(blk2, cur2),
        pl.BlockSpec(prev2_blk, prev2), pl.BlockSpec(prev2_blk, prev2),
        pl.BlockSpec(blk3, cur3), pl.BlockSpec(blk3, cur3), pl.BlockSpec(blk3, cur3),
        pl.BlockSpec(blk3, prev3), pl.BlockSpec(blk3, prev3),
        _resident(masks.shape),
    ]
    q1, k1, v1, q2, k2, v2, q3, k3, v3 = qkv
    return pl.pallas_call(
        _attn_kernel,
        out_shape=jax.ShapeDtypeStruct((batch, N_RES, seq // N_RES, GROUP_COL), BF16),
        grid=(batch, nchunk, 1 + 2 * spg),
        in_specs=in_specs,
        out_specs=pl.BlockSpec((1, STEP_BLOCKS, ATTN_BLK, GROUP_COL), lambda b, n, s: (b, step(s, 2), n, 0)),
        scratch_shapes=[pltpu.VMEM((N_RES, ATTN_BLK, GROUP_COL), F32)] * 3,
        compiler_params=pltpu.CompilerParams(dimension_semantics=("arbitrary",) * 3, vmem_limit_bytes=VMEM_LIMIT),
        name="dilated_attention",
    )(q1, k1, v1, k1, v1, q2, k2, v2, k2, v2, q3, k3, v3, k3, v3, masks)


def _mix_kernel(*refs, tiles_per_batch, n_casts, n_steps, jobs):
    x_ref, y_ref, w_ref, vec_ref, ws_ref, bsb_ref, pa_ref, pb_ref, pc_ref, wo_ref = refs[:10]
    cast_src = refs[10:10 + n_casts]
    o_ref = refs[10 + n_casts]
    cast_dst = refs[11 + n_casts:11 + 2 * n_casts]
    zc_ref, ys_ref = refs[11 + 2 * n_casts:]
    i = pl.program_id(0)

    @pl.when(i % tiles_per_batch == 0)
    def _():
        zc_ref[...] = jnp.zeros_like(zc_ref)

    _run_casts(jobs, cast_src, cast_dst, i, n_steps)

    x = x_ref[...]
    xb = x.astype(BF16)

    def vec(r):
        return vec_ref[r:r + 1, :]

    def proj(k):
        return jnp.dot(xb, w_ref[:, D_MODEL * k:D_MODEL * (k + 1)], preferred_element_type=F32)

    def out_proj(a, p_ref):
        return jnp.dot(a, p_ref[:, :D_MODEL], preferred_element_type=F32)

    v_pre = proj(7)
    u_pre = proj(6)

    for rho in range(N_RES):
        piece = y_ref[0, rho].astype(F32)
        for sl in range(GROUP_COL // LANES):
            ys_ref[sl, pl.ds(rho, ROWS_PER_RES, stride=N_RES), :] = piece[:, LANES * sl:LANES * (sl + 1)]
    yb = jnp.concatenate([ys_ref[sl] for sl in range(GROUP_COL // LANES)], axis=1).astype(BF16)
    m_attn = jax.nn.sigmoid(proj(1)) * out_proj(yb, pb_ref)

    z = proj(4) * proj(5)
    head = jnp.concatenate([zc_ref[...], z[:SUBLANES, :]], axis=0)
    zc_ref[...] = z[TILE - SUBLANES:, :]
    conv = vec(_P_CONV + 2) * z
    for shift in (1, 2):
        zs = jnp.concatenate([pltpu.roll(head, shift, 0)[SUBLANES:, :], pltpu.roll(z, shift, 0)[SUBLANES:, :]], axis=0)
        conv = conv + vec(_P_CONV + 2 - shift) * zs
    ya = (proj(3) * conv).astype(BF16)
    m = m_attn + jax.nn.sigmoid(proj(0)) * out_proj(ya, pa_ref)

    u = _gelu(u_pre)
    vb = _layer_norm(_gelu(v_pre), vec(_P_GLN_G), vec(_P_GLN_B)).astype(BF16)
    n_chunks = TILE // GMLP_CHUNK
    tril = (lax.broadcasted_iota(jnp.int32, (GMLP_CHUNK, GMLP_CHUNK), 0)
            >= lax.broadcasted_iota(jnp.int32, (GMLP_CHUNK, GMLP_CHUNK), 1))
    cols = []
    for g in range(GMLP_GROUPS):
        gl = slice(LANES * g, LANES * (g + 1))
        wg = jnp.where(tril, ws_ref[g], 0.0).astype(BF16)
        rhs = jnp.concatenate([vb[GMLP_CHUNK * c:GMLP_CHUNK * (c + 1), gl] for c in range(n_chunks)], axis=1)
        sp = jnp.dot(wg, rhs, preferred_element_type=F32)
        bias = bsb_ref[:, gl]
        cols.append(jnp.concatenate([sp[:, LANES * c:LANES * (c + 1)] + bias for c in range(n_chunks)], axis=0))
    yc = (u * jnp.concatenate(cols, axis=1)).astype(BF16)
    m = m + jax.nn.sigmoid(proj(2)) * out_proj(yc, pc_ref)

    mb16 = m.astype(BF16)
    for r in range(2):
        rs = slice(TILE // 2 * r, TILE // 2 * (r + 1))
        mixed = out_proj(mb16[rs, :], wo_ref)
        o_ref[rs, :] = _layer_norm(ALPHA * x[rs, :] + mixed, vec(_P_LN1_G), vec(_P_LN1_B))


def _mix_call(x2d, y_attn, w_rest, vecs, w_s, bsb, pa, pb, pc, wo, layer, seq, jobs):
    t = x2d.shape[0]
    nt = t // TILE
    tiles_per_batch = seq // TILE
    outs = pl.pallas_call(
        functools.partial(_mix_kernel, tiles_per_batch=tiles_per_batch, n_casts=len(jobs), n_steps=nt, jobs=jobs),
        out_shape=(jax.ShapeDtypeStruct((t, D_MODEL), F32),) + tuple(j.out_shape for j in jobs),
        grid=(nt,),
        in_specs=[pl.BlockSpec((TILE, D_MODEL), lambda i: (i, 0)),
                  pl.BlockSpec((1, N_RES, ROWS_PER_RES, GROUP_COL),
                               lambda i: (i // tiles_per_batch, 0, i % tiles_per_batch, 0)),
                  _resident(w_rest.shape), _layer_resident(vecs.shape, layer), _layer_resident(w_s.shape, layer),
                  _layer_resident(bsb.shape, layer)]
                 + [_resident(w.shape) for w in (pa, pb, pc, wo)] + [j.in_spec for j in jobs],
        out_specs=(pl.BlockSpec((TILE, D_MODEL), lambda i: (i, 0)),) + tuple(j.out_spec for j in jobs),
        scratch_shapes=[pltpu.VMEM((SUBLANES, D_MODEL), F32),
                        pltpu.VMEM((GROUP_COL // LANES, TILE, LANES), F32)],
        compiler_params=pltpu.CompilerParams(dimension_semantics=("arbitrary",), vmem_limit_bytes=VMEM_LIMIT),
        name="mixers_out_ln",
    )(x2d, y_attn, w_rest, vecs, w_s, bsb, pa, pb, pc, wo, *[j.src for j in jobs])
    return outs[0], outs[1:]


def _ffn_kernel(*refs, n_casts, n_steps, jobs):
    x_ref, wg_ref, wu_ref, wd_ref, vec_ref = refs[:5]
    cast_src = refs[5:5 + n_casts]
    o_ref = refs[5 + n_casts]
    cast_dst = refs[6 + n_casts:]
    _run_casts(jobs, cast_src, cast_dst, pl.program_id(0), n_steps)

    x = x_ref[...]
    xb = x.astype(BF16)
    gate = jnp.dot(xb, wg_ref[...], preferred_element_type=F32)
    up = jnp.dot(xb, wu_ref[...], preferred_element_type=F32)
    h = (jax.nn.silu(gate) * up).astype(BF16)
    for r in range(4):
        rs = slice(FFN_ROWS // 4 * r, FFN_ROWS // 4 * (r + 1))
        down = jnp.dot(h[rs, :], wd_ref[:, :D_MODEL], preferred_element_type=F32)
        o_ref[rs, :] = _layer_norm(ALPHA * x[rs, :] + down, vec_ref[_P_LN2_G:_P_LN2_G + 1, :],
                                   vec_ref[_P_LN2_B:_P_LN2_B + 1, :])


def _ffn_call(x2d, wg, wu, wd, vecs, layer, jobs):
    t = x2d.shape[0]
    nt = t // FFN_ROWS
    outs = pl.pallas_call(
        functools.partial(_ffn_kernel, n_casts=len(jobs), n_steps=nt, jobs=jobs),
        out_shape=(jax.ShapeDtypeStruct((t, D_MODEL), F32),) + tuple(j.out_shape for j in jobs),
        grid=(nt,),
        in_specs=[pl.BlockSpec((FFN_ROWS, D_MODEL), lambda i: (i, 0))] + [_resident(w.shape) for w in (wg, wu, wd)]
                 + [_layer_resident(vecs.shape, layer)] + [j.in_spec for j in jobs],
        out_specs=(pl.BlockSpec((FFN_ROWS, D_MODEL), lambda i: (i, 0)),) + tuple(j.out_spec for j in jobs),
        compiler_params=pltpu.CompilerParams(dimension_semantics=("arbitrary",), vmem_limit_bytes=VMEM_LIMIT),
        name="swiglu_ln",
    )(x2d, wg, wu, wd, vecs, *[j.src for j in jobs])
    return outs[0], outs[1:]


def _band_masks():
    u = np.arange(ATTN_BLK)
    stored = (
        (u % N_RES) * SUBLANES + u // N_RES,
        (u % 4) * 32 + u // 4,
        u,
    )
    out = np.empty((N_GROUPS, 2, 2 * ATTN_BLK, ATTN_BLK), np.float32)
    for g, st in enumerate(stored):
        orig = np.empty(ATTN_BLK, np.int64)
        orig[st] = u
        ki = orig[:, None]
        qi = orig[None, :]
        out[g, 0, :ATTN_BLK] = np.where(ki >= qi, 0.0, NEG)
        out[g, 1, :ATTN_BLK] = NEG
        out[g, :, ATTN_BLK:] = np.where(ki <= qi, 0.0, NEG)
    return out


def kernel(x, positions, w_in, conv_w, gmlp_ln_g, gmlp_ln_b, w_s, b_s, p_a, p_b, p_c, w_o, ln1_g, ln1_b, w_gate, w_up,
           w_down, ln2_g, ln2_b):
    batch, seq, d = x.shape
    t = batch * seq
    nt = t // TILE
    depth = w_in.shape[0]
    n_cast = _CAST_BLOCKS
    assert d == D_MODEL and seq % CHUNK == 0 and depth == DEPTH
    assert min(t // (QKV_SUB * TILE), t // FFN_ROWS, nt) >= n_cast

    pos = positions.reshape(nt, ROWS_PER_RES, N_RES).transpose(0, 2, 1).reshape(nt, 4, TILE // 4).astype(F32)
    pos4 = jnp.repeat(pos.transpose(0, 2, 1), HALF, axis=2).reshape(t // 4, LANES)
    inv_freq = ROPE_THETA ** (-jnp.arange(HALF, dtype=F32) / HALF)
    inv_row = jnp.tile(inv_freq, LANES // HALF).reshape(1, LANES)

    masks = jnp.asarray(_band_masks(), dtype=BF16)
    vecs = jnp.concatenate([conv_w] + [v[:, None, :] for v in (gmlp_ln_g, gmlp_ln_b, ln1_g, ln1_b, ln2_g, ln2_b)],
                           axis=1)
    bsb = jnp.repeat(jnp.swapaxes(b_s, 1, 2), GMLP_CHUNK, axis=2)

    rest_bw = _REST_COLS // n_cast
    gap = (_OFF_GMLP - _OFF_ATTN) // rest_bw

    def rest_col_block(i):
        return jnp.where(i < _OFF_ATTN // rest_bw, i, i + gap)

    def qkv_jobs(layer, n_blocks):
        return [_CastJob(w_in, layer, D_MODEL, _QKV_COLS, n_blocks, col_block=_OFF_ATTN // _QKV_COLS + g)
                for g in range(N_GROUPS)]

    w_groups = _cast_call(qkv_jobs(0, SUBLANES), SUBLANES)
    h = x.reshape(t, D_MODEL)
    for l in range(depth):
        jobs = [_CastJob(w_in, l, D_MODEL, _REST_COLS, n_cast, by_cols=True, col_map=rest_col_block),
                _CastJob(p_a, l, D_MODEL, D_MODEL, n_cast), _CastJob(p_b, l, GROUP_COL, D_MODEL, n_cast),
                _CastJob(p_c, l, D_MODEL, D_MODEL, n_cast), _CastJob(w_o, l, D_MODEL, D_MODEL, n_cast),
                _CastJob(w_down, l, D_FF, D_MODEL, n_cast)]
        qkv, (w_rest, pa, pb, pc, wo, wd) = _qkv_call(h, w_groups, pos4, inv_row, jobs)
        y_attn = _attn_call(qkv, masks, batch, seq)
        jobs = [_CastJob(w_gate, l, D_MODEL, D_FF, nt), _CastJob(w_up, l, D_MODEL, D_FF, nt)]
        h, (wg, wu) = _mix_call(h, y_attn, w_rest, vecs, w_s, bsb, pa, pb, pc, wo, l, seq, jobs)
        jobs = qkv_jobs(l + 1, n_cast) if l + 1 < depth else []
        h, w_next = _ffn_call(h, wg, wu, wd, vecs, l, jobs)
        w_groups = tuple(w_next)
    return h.reshape(batch, seq, D_MODEL)
```
